```python
import math
import jax, jax.numpy as jnp
from jax import lax
import numpy as np


D_MODEL = 4096
BATCH = 2
SEQ = 4096
DEPTH = 1

HEAD_DIM = 128
ATTN_WIDTH = D_MODEL // 2
N_Q_HEADS = ATTN_WIDTH // HEAD_DIM
N_KV_HEADS = N_Q_HEADS // 4
KV_WIDTH = N_KV_HEADS * HEAD_DIM
WINDOW = 128
BLOCK = 128
ROPE_DIM = HEAD_DIM // 4
ROPE_THETA = 500000.0

SSD_WIDTH = D_MODEL // 2
SSD_HEAD_DIM = 64
SSD_HEADS = SSD_WIDTH // SSD_HEAD_DIM
SSD_GROUPS = 4
D_STATE = 128
CONV_WIDTH = 5
CHUNK = 128
XBC_WIDTH = SSD_WIDTH + 2 * SSD_GROUPS * D_STATE

MIX_WIDTH = ATTN_WIDTH + SSD_WIDTH
IN_WIDTH = ATTN_WIDTH + 2 * KV_WIDTH + SSD_WIDTH + XBC_WIDTH + 2 * SSD_HEADS
D_FF = -(-8 * D_MODEL // (3 * 256)) * 256
LN_EPS = 1e-5
RMS_EPS = 1e-6

kernel_name = 'hybrid_swa_ssd_parallel_encoder_block'


def layer_norm(x, g, b):
    xf = x.astype(jnp.float32)
    mu = xf.mean(-1, keepdims=True)
    var = jnp.square(xf - mu).mean(-1, keepdims=True)
    return ((xf - mu) * lax.rsqrt(var + LN_EPS)).astype(x.dtype) * g + b


def rms_norm(x, g):
    xf = x.astype(jnp.float32)
    return (xf * lax.rsqrt(jnp.square(xf).mean(-1, keepdims=True) + RMS_EPS)).astype(x.dtype) * g


def partial_rope(t, positions):
    inv_freq = ROPE_THETA ** (-jnp.arange(0, ROPE_DIM, 2, dtype=jnp.float32) / ROPE_DIM)
    ang = positions.astype(jnp.float32)[..., None] * inv_freq
    cos = jnp.cos(ang)[:, :, None, :]
    sin = jnp.sin(ang)[:, :, None, :]
    rot = t[..., :ROPE_DIM].astype(jnp.float32)
    x1, x2 = rot[..., :ROPE_DIM // 2], rot[..., ROPE_DIM // 2:]
    rot = jnp.concatenate([x1 * cos - x2 * sin, x2 * cos + x1 * sin], axis=-1)
    return jnp.concatenate([rot.astype(t.dtype), t[..., ROPE_DIM:]], axis=-1)


def windowed_attention(q, k, v, sink):
    bsz, seq = q.shape[0], q.shape[1]
    nb = seq // BLOCK
    grp = N_Q_HEADS // N_KV_HEADS
    qb = q.reshape(bsz, nb, BLOCK, N_KV_HEADS, grp, HEAD_DIM)

    def band(t):
        tp = jnp.pad(t, ((0, 0), (BLOCK, BLOCK), (0, 0), (0, 0)))
        tb = tp.reshape(bsz, nb + 2, BLOCK, N_KV_HEADS, HEAD_DIM)
        return jnp.concatenate([tb[:, :-2], tb[:, 1:-1], tb[:, 2:]], axis=2)

    kw, vw = band(k), band(v)
    scores = jnp.einsum('bnqkgd,bnskd->bnkgqs', qb, kw).astype(jnp.float32) * (HEAD_DIM ** -0.5)
    qi = jnp.arange(BLOCK)[:, None]
    sj = jnp.arange(3 * BLOCK)[None, :]
    in_window = jnp.abs(sj - BLOCK - qi) <= WINDOW
    kpos = jnp.arange(nb)[:, None] * BLOCK + jnp.arange(3 * BLOCK)[None, :] - BLOCK
    valid = (kpos >= 0) & (kpos < seq)
    mask = in_window[None] & valid[:, None, :]
    scores = jnp.where(mask[None, :, None, None], scores, -jnp.inf)
    sink_l = sink.astype(jnp.float32).reshape(N_KV_HEADS, grp)[None, None, :, :, None, None]
    m = jnp.maximum(scores.max(-1, keepdims=True), sink_l)
    p = jnp.exp(scores - m)
    denom = p.sum(-1, keepdims=True) + jnp.exp(sink_l - m)
    probs = (p / denom).astype(v.dtype)
    out = jnp.einsum('bnkgqs,bnskd->bnqkgd', probs, vw)
    return out.reshape(bsz, seq, ATTN_WIDTH)


def depthwise_conv(u, w, b):
    pad = CONV_WIDTH // 2
    out = lax.conv_general_dilated(u, w[:, None, :], (1,), [(pad, pad)],
                                   dimension_numbers=('NWC', 'WIO', 'NWC'),
                                   feature_group_count=u.shape[-1])
    return out + b


def ssd_scan(xh, dt, a, bm, cm):
    bsz, seq = xh.shape[0], xh.shape[1]
    nc = seq // CHUNK
    rep = SSD_HEADS // SSD_GROUPS
    x = xh.astype(jnp.float32).reshape(bsz, nc, CHUNK, SSD_GROUPS, rep, SSD_HEAD_DIM)
    dtc = dt.reshape(bsz, nc, CHUNK, SSD_GROUPS, rep)
    bc = bm.astype(jnp.float32).reshape(bsz, nc, CHUNK, SSD_GROUPS, D_STATE)
    cc = cm.astype(jnp.float32).reshape(bsz, nc, CHUNK, SSD_GROUPS, D_STATE)
    xdt = x * dtc[..., None]
    a_dt = jnp.moveaxis(dtc * a.reshape(SSD_GROUPS, rep), 2, -1)
    a_cum = jnp.cumsum(a_dt, axis=-1)
    tri = jnp.tril(jnp.ones((CHUNK, CHUNK), dtype=bool))
    seg = a_cum[..., :, None] - a_cum[..., None, :]
    decay = jnp.exp(jnp.where(tri, seg, -jnp.inf))
    cb = jnp.einsum('bclgn,bcsgn->bcgls', cc, bc)
    y_diag = jnp.einsum('bcgls,bcgrls,bcsgrp->bclgrp', cb, decay, xdt)
    decay_states = jnp.exp(a_cum[..., -1:] - a_cum)
    states = jnp.einsum('bclgn,bcgrl,bclgrp->bcgrpn', bc, decay_states, xdt)
    chunk_decay = jnp.exp(a_cum[..., -1])

    def step(h, inp):
        st, dec = inp
        return h * dec[..., None, None] + st, h

    init = jnp.zeros((bsz, SSD_GROUPS, rep, SSD_HEAD_DIM, D_STATE), jnp.float32)
    _, prev = lax.scan(step, init, (jnp.moveaxis(states, 1, 0), jnp.moveaxis(chunk_decay, 1, 0)))
    prev = jnp.moveaxis(prev, 0, 1)
    y_off = jnp.einsum('bclgn,bcgrpn,bcgrl->bclgrp', cc, prev, jnp.exp(a_cum))
    return (y_diag + y_off).reshape(bsz, seq, SSD_HEADS, SSD_HEAD_DIM)


def hybrid_mixer(h, positions, w_in, conv_w, conv_b, attn_sink, a_log_fwd, a_log_bwd,
                 dt_bias_fwd, dt_bias_bwd, ssd_d, ssd_norm_w, attn_norm_w, w_out):
    bsz, seq = h.shape[0], h.shape[1]
    proj = h @ w_in
    cuts = [ATTN_WIDTH, ATTN_WIDTH + KV_WIDTH, ATTN_WIDTH + 2 * KV_WIDTH,
            ATTN_WIDTH + 2 * KV_WIDTH + SSD_WIDTH,
            ATTN_WIDTH + 2 * KV_WIDTH + SSD_WIDTH + XBC_WIDTH]
    q, k, v, z, xbc, dt_raw = jnp.split(proj, cuts, axis=-1)
    q = partial_rope(q.reshape(bsz, seq, N_Q_HEADS, HEAD_DIM), positions)
    k = partial_rope(k.reshape(bsz, seq, N_KV_HEADS, HEAD_DIM), positions)
    v = v.reshape(bsz, seq, N_KV_HEADS, HEAD_DIM)
    attn = rms_norm(windowed_attention(q, k, v, attn_sink), attn_norm_w)
    xbc = jax.nn.silu(depthwise_conv(xbc, conv_w, conv_b))
    xs, bm, cm = jnp.split(xbc, [SSD_WIDTH, SSD_WIDTH + SSD_GROUPS * D_STATE], axis=-1)
    xh = xs.reshape(bsz, seq, SSD_HEADS, SSD_HEAD_DIM)
    bm = bm.reshape(bsz, seq, SSD_GROUPS, D_STATE)
    cm = cm.reshape(bsz, seq, SSD_GROUPS, D_STATE)
    dt_raw = dt_raw.astype(jnp.float32)
    dt_f = jax.nn.softplus(dt_raw[..., :SSD_HEADS] + dt_bias_fwd.astype(jnp.float32))
    dt_b = jax.nn.softplus(dt_raw[..., SSD_HEADS:] + dt_bias_bwd.astype(jnp.float32))
    a_f = -jnp.exp(a_log_fwd.astype(jnp.float32))
    a_b = -jnp.exp(a_log_bwd.astype(jnp.float32))
    flip = lambda t: jnp.flip(t, axis=1)
    y = ssd_scan(xh, dt_f, a_f, bm, cm) + flip(ssd_scan(flip(xh), flip(dt_b), a_b, flip(bm), flip(cm)))
    y = y + ssd_d.astype(jnp.float32)[:, None] * xh.astype(jnp.float32)
    y = y.reshape(bsz, seq, SSD_WIDTH) * jax.nn.silu(z.astype(jnp.float32))
    yg = y.reshape(bsz, seq, SSD_GROUPS, SSD_WIDTH // SSD_GROUPS)
    yg = yg * lax.rsqrt(jnp.square(yg).mean(-1, keepdims=True) + RMS_EPS)
    ssd = yg.reshape(bsz, seq, SSD_WIDTH).astype(h.dtype) * ssd_norm_w
    return jnp.concatenate([attn, ssd], axis=-1) @ w_out


def swiglu(h, w_gate, w_up, w_down):
    return (jax.nn.silu(h @ w_gate) * (h @ w_up)) @ w_down


def setup_inputs(seed: int = 0) -> dict:
    key = jax.random.key(seed)
    ks = jax.random.split(key, 24)
    f32 = jnp.float32
    beta = (8 * DEPTH) ** -0.25

    def nrm(k, shape, scale):
        return jax.random.normal(k, shape, f32) * scale

    def make_dt_bias(k):
        dt = jnp.exp(jax.random.uniform(k, (DEPTH, SSD_HEADS), f32, math.log(1e-3), math.log(1e-1)))
        return dt + jnp.log(-jnp.expm1(-dt))

    x = nrm(ks[0], (BATCH, SEQ, D_MODEL), 1.0)
    c = nrm(ks[1], (BATCH, D_MODEL), 1.0)
    positions = jnp.broadcast_to(jnp.arange(SEQ, dtype=jnp.int32), (BATCH, SEQ))
    w_ada = nrm(ks[2], (DEPTH, D_MODEL, 6 * D_MODEL), 0.5 * D_MODEL ** -0.5)
    b_ada = nrm(ks[3], (DEPTH, 6 * D_MODEL), 0.02)
    col_scale = jnp.concatenate([jnp.ones((ATTN_WIDTH + KV_WIDTH,), f32),
                                 jnp.full((KV_WIDTH,), beta, f32),
                                 jnp.ones((IN_WIDTH - ATTN_WIDTH - 2 * KV_WIDTH,), f32)])
    w_in = nrm(ks[4], (DEPTH, D_MODEL, IN_WIDTH), D_MODEL ** -0.5) * col_scale
    conv_w = nrm(ks[5], (DEPTH, CONV_WIDTH, XBC_WIDTH), CONV_WIDTH ** -0.5)
    conv_b = nrm(ks[6], (DEPTH, XBC_WIDTH), 0.02)
    attn_sink = nrm(ks[7], (DEPTH, N_Q_HEADS), 0.5)
    a_log_fwd = jnp.log(jax.random.uniform(ks[8], (DEPTH, SSD_HEADS), f32, 1.0, 16.0))
    a_log_bwd = jnp.log(jax.random.uniform(ks[9], (DEPTH, SSD_HEADS), f32, 1.0, 16.0))
    dt_bias_fwd = make_dt_bias(ks[10])
    dt_bias_bwd = make_dt_bias(ks[11])
    ssd_d = 1.0 + nrm(ks[12], (DEPTH, SSD_HEADS), 0.1)
    ssd_norm_w = 1.0 + nrm(ks[13], (DEPTH, SSD_WIDTH), 0.02)
    attn_norm_w = 1.0 + nrm(ks[14], (DEPTH, ATTN_WIDTH), 0.02)
    w_out = nrm(ks[15], (DEPTH, MIX_WIDTH, D_MODEL), beta * MIX_WIDTH ** -0.5)
    ln1_g = 1.0 + nrm(ks[16], (DEPTH, D_MODEL), 0.02)
    ln1_b = nrm(ks[17], (DEPTH, D_MODEL), 0.02)
    w_gate = nrm(ks[18], (DEPTH, D_MODEL, D_FF), beta * D_MODEL ** -0.5)
    w_up = nrm(ks[19], (DEPTH, D_MODEL, D_FF), beta * D_MODEL ** -0.5)
    w_down = nrm(ks[20], (DEPTH, D_FF, D_MODEL), beta * D_FF ** -0.5)
    ln2_g = 1.0 + nrm(ks[21], (DEPTH, D_MODEL), 0.02)
    ln2_b = nrm(ks[22], (DEPTH, D_MODEL), 0.02)
    return {'x': x, 'c': c, 'positions': positions, 'w_ada': w_ada, 'b_ada': b_ada,
            'w_in': w_in, 'conv_w': conv_w, 'conv_b': conv_b, 'attn_sink': attn_sink,
            'a_log_fwd': a_log_fwd, 'a_log_bwd': a_log_bwd,
            'dt_bias_fwd': dt_bias_fwd, 'dt_bias_bwd': dt_bias_bwd,
            'ssd_d': ssd_d, 'ssd_norm_w': ssd_norm_w, 'attn_norm_w': attn_norm_w,
            'w_out': w_out, 'ln1_g': ln1_g, 'ln1_b': ln1_b,
            'w_gate': w_gate, 'w_up': w_up, 'w_down': w_down,
            'ln2_g': ln2_g, 'ln2_b': ln2_b}


def reference(x, c, positions, w_ada, b_ada, w_in, conv_w, conv_b, attn_sink,
              a_log_fwd, a_log_bwd, dt_bias_fwd, dt_bias_bwd, ssd_d, ssd_norm_w,
              attn_norm_w, w_out, ln1_g, ln1_b, w_gate, w_up, w_down, ln2_g, ln2_b):
    alpha = (2 * DEPTH) ** 0.25
    cond = jax.nn.silu(c)
    for l in range(DEPTH):
        mod = cond @ w_ada[l] + b_ada[l]
        sh1, sc1, g1, sh2, sc2, g2 = [m[:, None, :] for m in jnp.split(mod, 6, axis=-1)]
        h = x * (1.0 + sc1) + sh1
        mix = hybrid_mixer(h, positions, w_in[l], conv_w[l], conv_b[l], attn_sink[l],
                           a_log_fwd[l], a_log_bwd[l], dt_bias_fwd[l], dt_bias_bwd[l],
                           ssd_d[l], ssd_norm_w[l], attn_norm_w[l], w_out[l])
        x = layer_norm(alpha * x + g1 * mix, ln1_g[l], ln1_b[l])
        h = x * (1.0 + sc2) + sh2
        x = layer_norm(alpha * x + g2 * swiglu(h, w_gate[l], w_up[l], w_down[l]), ln2_g[l], ln2_b[l])
    return x
```

```python
import functools

import jax
import jax.numpy as jnp
from jax import lax
from jax.experimental import pallas as pl
from jax.experimental.pallas import tpu as pltpu

F32 = jnp.float32
BF16 = jnp.bfloat16

D_MODEL = 4096
HEAD_DIM = 128
ATTN_WIDTH = D_MODEL // 2
N_Q_HEADS = ATTN_WIDTH // HEAD_DIM
N_KV_HEADS = N_Q_HEADS // 4
Q_PER_KV = N_Q_HEADS // N_KV_HEADS
KV_WIDTH = N_KV_HEADS * HEAD_DIM
WINDOW = 128
BLOCK = 128
ROPE_DIM = HEAD_DIM // 4
ROPE_THETA = 500000.0
SSD_WIDTH = D_MODEL // 2
SSD_HEAD_DIM = 64
SSD_HEADS = SSD_WIDTH // SSD_HEAD_DIM
SSD_GROUPS = 4
HEADS_PER_GROUP = SSD_HEADS // SSD_GROUPS
GROUP_WIDTH = SSD_WIDTH // SSD_GROUPS
D_STATE = 128
CONV_WIDTH = 5
CHUNK = 128
XBC_WIDTH = SSD_WIDTH + 2 * SSD_GROUPS * D_STATE
MAIN_WIDTH = ATTN_WIDTH + 2 * KV_WIDTH + SSD_WIDTH + XBC_WIDTH
D_FF = -(-8 * D_MODEL // (3 * 256)) * 256
LN_EPS = 1e-5
RMS_EPS = 1e-6
ALPHA = 2.0 ** 0.25

Q_OFF = 0
K_OFF = ATTN_WIDTH
V_OFF = K_OFF + KV_WIDTH
Z_OFF = V_OFF + KV_WIDTH
XBC_OFF = Z_OFF + SSD_WIDTH
assert MAIN_WIDTH == XBC_OFF + XBC_WIDTH

V7X_VMEM_BYTES = 64 * 1024 * 1024
VMEM_LIMIT = V7X_VMEM_BYTES * 7 // 8
LANES = 128
DT_LANES = LANES
HALO_ROWS = 16


def _cparams(*sem):
    return pltpu.CompilerParams(dimension_semantics=sem, vmem_limit_bytes=VMEM_LIMIT)


def _silu(v):
    return v / (1.0 + jnp.exp(-v))


def _ada_kernel(ct_ref, w_ref, b_ref, o_ref):
    k = pl.program_id(1)
    nb = o_ref.shape[0]

    @pl.when(k == 0)
    def _():
        o_ref[...] = jnp.broadcast_to(b_ref[...], o_ref.shape)

    cond = _silu(ct_ref[...])
    w = w_ref[...]
    rows = [jnp.sum(w * cond[:, b:b + 1], axis=0, keepdims=True) for b in range(nb)]
    o_ref[...] += jnp.concatenate(rows, axis=0)


def _ada(c, w_ada, b_ada, tk=512, tn=2048):
    bsz, d = c.shape
    n = w_ada.shape[1]
    return pl.pallas_call(
        _ada_kernel,
        grid=(n // tn, d // tk),
        in_specs=[pl.BlockSpec((tk, bsz), lambda j, k: (k, 0)),
                  pl.BlockSpec((tk, tn), lambda j, k: (k, j)),
                  pl.BlockSpec((1, tn), lambda j, k: (0, j))],
        out_specs=pl.BlockSpec((bsz, tn), lambda j, k: (0, j)),
        out_shape=jax.ShapeDtypeStruct((bsz, n), F32),
        compiler_params=_cparams("parallel", "arbitrary"),
        name="ada_mod",
    )(c.T, w_ada, b_ada.reshape(1, n))


def _inproj_kernel(x_ref, sc_ref, sh_ref, w_ref, wdt_ref, o_ref, odt_ref, h_ref):
    @pl.when(pl.program_id(1) == 0)
    def _():
        h = (x_ref[...] * (1.0 + sc_ref[0]) + sh_ref[0]).astype(BF16)
        h_ref[...] = h
        odt_ref[...] = jnp.dot(h, wdt_ref[...], preferred_element_type=F32)

    o_ref[...] = jnp.dot(h_ref[...], w_ref[...], preferred_element_type=F32).astype(o_ref.dtype)


def _inproj(x2, scale, shift, w_main, w_dt, seq, tm=512, tn=1024):
    m, d = x2.shape
    n = w_main.shape[1]
    tm = min(tm, seq)
    per_batch = seq // tm
    return pl.pallas_call(
        _inproj_kernel,
        grid=(m // tm, n // tn),
        in_specs=[pl.BlockSpec((tm, d), lambda i, j: (i, 0)),
                  pl.BlockSpec((1, 1, d), lambda i, j: (i // per_batch, 0, 0)),
                  pl.BlockSpec((1, 1, d), lambda i, j: (i // per_batch, 0, 0)),
                  pl.BlockSpec((d, tn), lambda i, j: (0, j)),
                  pl.BlockSpec((d, DT_LANES), lambda i, j: (0, 0))],
        out_specs=[pl.BlockSpec((tm, tn), lambda i, j: (i, j)),
                   pl.BlockSpec((tm, DT_LANES), lambda i, j: (i, 0))],
        out_shape=[jax.ShapeDtypeStruct((m, n), BF16),
                   jax.ShapeDtypeStruct((m, DT_LANES), F32)],
        scratch_shapes=[pltpu.VMEM((tm, d), BF16)],
        compiler_params=_cparams("parallel", "arbitrary"),
        name="in_proj",
    )(x2, scale, shift, w_main, w_dt)


def _rope_kernel(q_ref, k_ref, c_ref, s1_ref, s2_ref, qo_ref, ko_ref):
    cos, s1, s2 = c_ref[0], s1_ref[0], s2_ref[0]
    half = ROPE_DIM // 2

    def rot(t):
        t = t.astype(F32)
        return (t * cos + pltpu.roll(t, HEAD_DIM - half, 1) * s1 + pltpu.roll(t, half, 1) * s2)

    for h in range(N_Q_HEADS):
        sl = slice(h * HEAD_DIM, (h + 1) * HEAD_DIM)
        qo_ref[0, :, sl] = rot(q_ref[0, :, sl]).astype(qo_ref.dtype)
    for h in range(N_KV_HEADS):
        sl = slice(h * HEAD_DIM, (h + 1) * HEAD_DIM)
        ko_ref[0, :, sl] = rot(k_ref[0, :, sl]).astype(ko_ref.dtype)


def _rope(proj3, cos_t, s1_t, s2_t, ts=256):
    bsz, seq, _ = proj3.shape
    ts = min(ts, seq)
    tab = pl.BlockSpec((1, ts, HEAD_DIM), lambda b, i: (b, i, 0))
    return pl.pallas_call(
        _rope_kernel,
        grid=(bsz, seq // ts),
        in_specs=[pl.BlockSpec((1, ts, ATTN_WIDTH), lambda b, i: (b, i, Q_OFF // ATTN_WIDTH)),
                  pl.BlockSpec((1, ts, KV_WIDTH), lambda b, i: (b, i, K_OFF // KV_WIDTH)),
                  tab, tab, tab],
        out_specs=[pl.BlockSpec((1, ts, ATTN_WIDTH), lambda b, i: (b, i, 0)),
                   pl.BlockSpec((1, ts, KV_WIDTH), lambda b, i: (b, i, 0))],
        out_shape=[jax.ShapeDtypeStruct((bsz, seq, ATTN_WIDTH), BF16),
                   jax.ShapeDtypeStruct((bsz, seq, KV_WIDTH), BF16)],
        compiler_params=_cparams("parallel", "parallel"),
        name="rope",
    )(proj3, proj3, cos_t, s1_t, s2_t)


def _attn_kernel(sink_ref, q_ref, kp_ref, kc_ref, kn_ref, vp_ref, vc_ref, vn_ref, nw_ref,
                 o_ref, acc_ref, *, seq):
    i = pl.program_id(1)
    qi = lax.broadcasted_iota(jnp.int32, (BLOCK, 3 * BLOCK), 0)
    sj = lax.broadcasted_iota(jnp.int32, (BLOCK, 3 * BLOCK), 1)
    kpos = i * BLOCK + sj - BLOCK
    keep = jnp.where(jnp.abs(sj - BLOCK - qi) <= WINDOW,
                     jnp.where(kpos >= 0, jnp.where(kpos < seq, 1, 0), 0), 0)
    bias = jnp.where(keep == 1, 0.0, -jnp.inf).astype(F32)
    scale = HEAD_DIM ** -0.5
    ssq = jnp.zeros((BLOCK, 1), F32)
    for kv in range(N_KV_HEADS):
        ksl = slice(kv * HEAD_DIM, (kv + 1) * HEAD_DIM)
        kwin = jnp.concatenate([kp_ref[0, :, ksl], kc_ref[0, :, ksl], kn_ref[0, :, ksl]], axis=0)
        vwin = jnp.concatenate([vp_ref[0, :, ksl], vc_ref[0, :, ksl], vn_ref[0, :, ksl]], axis=0)
        q4 = jnp.concatenate(
            [q_ref[0, :, (kv * Q_PER_KV + g) * HEAD_DIM:(kv * Q_PER_KV + g + 1) * HEAD_DIM]
             for g in range(Q_PER_KV)], axis=0)
        s = lax.dot_general(q4, kwin, (((1,), (1,)), ((), ())), preferred_element_type=F32)
        probs = []
        for g in range(Q_PER_KV):
            sink = sink_ref[kv * Q_PER_KV + g]
            sg = s[g * BLOCK:(g + 1) * BLOCK] * scale + bias
            m = jnp.maximum(jnp.max(sg, axis=-1, keepdims=True), sink)
            p = jnp.exp(sg - m)
            denom = jnp.sum(p, axis=-1, keepdims=True) + jnp.exp(sink - m)
            probs.append((p / denom).astype(BF16))
        o = jnp.dot(jnp.concatenate(probs, axis=0), vwin, preferred_element_type=F32)
        for g in range(Q_PER_KV):
            og = o[g * BLOCK:(g + 1) * BLOCK]
            h = kv * Q_PER_KV + g
            acc_ref[:, h * HEAD_DIM:(h + 1) * HEAD_DIM] = og
            ssq = ssq + jnp.sum(og * og, axis=-1, keepdims=True)
    inv = lax.rsqrt(ssq / ATTN_WIDTH + RMS_EPS)
    o_ref[0] = (acc_ref[...] * inv * nw_ref[...]).astype(o_ref.dtype)


def _attention(qr, kr, proj3, sink, norm_w):
    bsz, seq, _ = qr.shape
    nb = seq // BLOCK
    vcol = V_OFF // KV_WIDTH

    def kv_specs(col):
        return [pl.BlockSpec((1, BLOCK, KV_WIDTH), lambda b, i: (b, jnp.maximum(i - 1, 0), col)),
                pl.BlockSpec((1, BLOCK, KV_WIDTH), lambda b, i: (b, i, col)),
                pl.BlockSpec((1, BLOCK, KV_WIDTH), lambda b, i: (b, jnp.minimum(i + 1, nb - 1), col))]

    return pl.pallas_call(
        functools.partial(_attn_kernel, seq=seq),
        grid=(bsz, nb),
        in_specs=[pl.BlockSpec(memory_space=pltpu.SMEM),
                  pl.BlockSpec((1, BLOCK, ATTN_WIDTH), lambda b, i: (b, i, 0))]
                 + kv_specs(0) + kv_specs(vcol)
                 + [pl.BlockSpec((1, ATTN_WIDTH), lambda b, i: (0, 0))],
        out_specs=pl.BlockSpec((1, BLOCK, ATTN_WIDTH), lambda b, i: (b, i, 0)),
        out_shape=jax.ShapeDtypeStruct((bsz, seq, ATTN_WIDTH), BF16),
        scratch_shapes=[pltpu.VMEM((BLOCK, ATTN_WIDTH), F32)],
        compiler_params=_cparams("parallel", "parallel"),
        name="band_attn",
    )(sink, qr, kr, kr, kr, proj3, proj3, proj3, norm_w.reshape(1, ATTN_WIDTH))


def _conv_kernel(prev_ref, cur_ref, next_ref, w_ref, b_ref, o_ref):
    i = pl.program_id(1)
    last = pl.num_programs(1) - 1
    ts = cur_ref.shape[1]
    pad = CONV_WIDTH // 2
    prev = jnp.where(i > 0, prev_ref[0].astype(F32), 0.0)
    nxt = jnp.where(i < last, next_ref[0].astype(F32), 0.0)
    ext = jnp.concatenate([prev, cur_ref[0].astype(F32), nxt], axis=0)
    rows = ext.shape[0]
    w = w_ref[...]
    acc = jnp.broadcast_to(b_ref[...], (ts, ext.shape[1]))
    for j in range(CONV_WIDTH):
        d = j - pad
        shifted = ext if d == 0 else pltpu.roll(ext, (-d) % rows, 0)
        acc = acc + shifted[HALO_ROWS:HALO_ROWS + ts] * w[j:j + 1, :]
    o_ref[0] = _silu(acc).astype(o_ref.dtype)


def _conv(proj3, conv_w8, conv_b, ts=512, tc=1024):
    bsz, seq, _ = proj3.shape
    ts = min(ts, seq)
    col0 = XBC_OFF // tc
    hb = ts // HALO_ROWS
    nhalo = seq // HALO_ROWS
    return pl.pallas_call(
        _conv_kernel,
        grid=(bsz, seq // ts, XBC_WIDTH // tc),
        in_specs=[pl.BlockSpec((1, HALO_ROWS, tc), lambda b, i, j: (b, jnp.maximum(i * hb - 1, 0), col0 + j)),
                  pl.BlockSpec((1, ts, tc), lambda b, i, j: (b, i, col0 + j)),
                  pl.BlockSpec((1, HALO_ROWS, tc),
                               lambda b, i, j: (b, jnp.minimum((i + 1) * hb, nhalo - 1), col0 + j)),
                  pl.BlockSpec((8, tc), lambda b, i, j: (0, j)),
                  pl.BlockSpec((1, tc), lambda b, i, j: (0, j))],
        out_specs=pl.BlockSpec((1, ts, tc), lambda b, i, j: (b, i, j)),
        out_shape=jax.ShapeDtypeStruct((bsz, seq, XBC_WIDTH), BF16),
        compiler_params=_cparams("parallel", "parallel", "parallel"),
        name="dwconv_silu",
    )(proj3, proj3, proj3, conv_w8, conv_b.reshape(1, XBC_WIDTH))


def _bf16_pieces(v, parts):
    pieces = []
    rem = v
    for p in range(parts):
        piece = rem.astype(BF16)
        pieces.append(piece)
        if p + 1 < parts:
            rem = rem - piece.astype(F32)
    return pieces


def _dot_f32(a, b):
    return jnp.dot(a, b, preferred_element_type=F32)


def _ssd_kernel(xs_ref, b_ref, c_ref, dt_ref, par_ref, y_ref, state_ref, *, reverse):
    L = CHUNK

    @pl.when(pl.program_id(1) == 0)
    def _():
        state_ref[...] = jnp.zeros_like(state_ref)

    row = lax.broadcasted_iota(jnp.int32, (L, L), 0)
    col = lax.broadcasted_iota(jnp.int32, (L, L), 1)
    causal = (col >= row) if reverse else (col <= row)
    tri_b = jnp.where(causal, 1.0, 0.0).astype(BF16)

    xb = dt_ref[0] + par_ref[0:1, :]
    dt = jnp.maximum(xb, 0.0) + jnp.log1p(jnp.exp(-jnp.abs(xb)))
    adt = dt * (-jnp.exp(par_ref[1:2, :]))
    acum = sum(_dot_f32(tri_b, piece) for piece in _bf16_pieces(adt, 3))
    acum_t = acum.T
    edge = acum[0:1, :] if reverse else acum[L - 1:L, :]
    stack = jnp.concatenate([dt, jnp.exp(acum), jnp.exp(edge - acum)], axis=0)

    lane = lax.broadcasted_iota(jnp.int32, (L, LANES), 1)
    head_off = SSD_HEADS if reverse else 0
    for g in range(SSD_GROUPS):
        base = head_off + g * HEADS_PER_GROUP
        gsl = slice(g * GROUP_WIDTH, (g + 1) * GROUP_WIDTH)
        er = lax.broadcasted_iota(jnp.int32, (LANES, GROUP_WIDTH), 0)
        ec = lax.broadcasted_iota(jnp.int32, (LANES, GROUP_WIDTH), 1) // SSD_HEAD_DIM
        expand_b = jnp.where(er == ec + base, 1.0, 0.0).astype(BF16)
        ex = sum(_dot_f32(piece, expand_b) for piece in _bf16_pieces(stack, 2))
        dt_x, eac_x, ds_x = ex[0:L], ex[L:2 * L], ex[2 * L:3 * L]
        xdt = xs_ref[0, :, gsl].astype(F32) * dt_x
        bm = b_ref[0, :, g * D_STATE:(g + 1) * D_STATE]
        cm = c_ref[0, :, g * D_STATE:(g + 1) * D_STATE]
        cb = lax.dot_general(cm, bm, (((1,), (1,)), ((), ())), preferred_element_type=F32)
        ys = []
        for pair in range(HEADS_PER_GROUP // 2):
            ms = []
            for r in (2 * pair, 2 * pair + 1):
                hl = base + r
                seg = acum[:, hl:hl + 1] - acum_t[hl:hl + 1, :]
                ms.append((cb * jnp.exp(jnp.where(causal, seg, -jnp.inf))).astype(BF16))
            xp = xdt[:, pair * LANES:(pair + 1) * LANES]
            rhs = jnp.concatenate([jnp.where(lane < SSD_HEAD_DIM, xp, 0.0),
                                   jnp.where(lane >= SSD_HEAD_DIM, xp, 0.0)], axis=0).astype(BF16)
            ys.append(jnp.dot(jnp.concatenate(ms, axis=1), rhs, preferred_element_type=F32))
        y_diag = jnp.concatenate(ys, axis=1)
        st = state_ref[g]
        y_off = jnp.dot(cm, st.astype(BF16), preferred_element_type=F32) * eac_x
        y_ref[0, :, gsl] = y_diag + y_off
        new = lax.dot_general(bm, (xdt * ds_x).astype(BF16), (((0,), (0,)), ((), ())),
                              preferred_element_type=F32)
        cdec = eac_x[0:1, :] if reverse else eac_x[L - 1:L, :]
        state_ref[g] = st * cdec + new


def _ssd_scan(xbc, dt_raw3, par, reverse):
    bsz, seq, _ = xbc.shape
    nc = seq // CHUNK
    bc_w = SSD_GROUPS * D_STATE
    cidx = (lambda c: nc - 1 - c) if reverse else (lambda c: c)
    return pl.pallas_call(
        functools.partial(_ssd_kernel, reverse=reverse),
        grid=(bsz, nc),
        in_specs=[pl.BlockSpec((1, CHUNK, SSD_WIDTH), lambda b, c: (b, cidx(c), 0)),
                  pl.BlockSpec((1, CHUNK, bc_w), lambda b, c: (b, cidx(c), SSD_WIDTH // bc_w)),
                  pl.BlockSpec((1, CHUNK, bc_w), lambda b, c: (b, cidx(c), SSD_WIDTH // bc_w + 1)),
                  pl.BlockSpec((1, CHUNK, DT_LANES), lambda b, c: (b, cidx(c), 0)),
                  pl.BlockSpec((8, LANES), lambda b, c: (0, 0))],
        out_specs=pl.BlockSpec((1, CHUNK, SSD_WIDTH), lambda b, c: (b, cidx(c), 0)),
        out_shape=jax.ShapeDtypeStruct((bsz, seq, SSD_WIDTH), F32),
        scratch_shapes=[pltpu.VMEM((SSD_GROUPS, D_STATE, GROUP_WIDTH), F32)],
        compiler_params=_cparams("parallel", "arbitrary"),
        name="ssd_bwd" if reverse else "ssd_fwd",
    )(xbc, xbc, xbc, dt_raw3, par)


def _gate_kernel(yf_ref, yb_ref, xs_ref, z_ref, d_ref, nw_ref, o_ref):
    y = yf_ref[0] + yb_ref[0] + d_ref[0] * xs_ref[0].astype(F32)
    y = y * _silu(z_ref[0].astype(F32))
    y = y * lax.rsqrt(jnp.mean(y * y, axis=-1, keepdims=True) + RMS_EPS)
    o_ref[0] = (y * nw_ref[0]).astype(o_ref.dtype)


def _gate(y_f, y_b, xbc, proj3, d_x, norm_w, ts=512):
    bsz, seq, _ = y_f.shape
    ts = min(ts, seq)
    blk = lambda col0: pl.BlockSpec((1, ts, GROUP_WIDTH), lambda b, i, g: (b, i, col0 + g))
    par = pl.BlockSpec((1, 1, GROUP_WIDTH), lambda b, i, g: (g, 0, 0))
    return pl.pallas_call(
        _gate_kernel,
        grid=(bsz, seq // ts, SSD_GROUPS),
        in_specs=[blk(0), blk(0), blk(0), blk(Z_OFF // GROUP_WIDTH), par, par],
        out_specs=blk(0),
        out_shape=jax.ShapeDtypeStruct((bsz, seq, SSD_WIDTH), BF16),
        compiler_params=_cparams("parallel", "parallel", "parallel"),
        name="ssd_gate_norm",
    )(y_f, y_b, xbc, proj3, d_x, norm_w)


def _outproj_kernel(a_ref, s_ref, wa_ref, ws_ref, o_ref):
    o_ref[...] = (jnp.dot(a_ref[...], wa_ref[...], preferred_element_type=F32)
                  + jnp.dot(s_ref[...], ws_ref[...], preferred_element_type=F32))


def _outproj(attn2, ssd2, w_out, tm=1024, tn=1024):
    m, ka = attn2.shape
    n = w_out.shape[1]
    tm = min(tm, m)
    return pl.pallas_call(
        _outproj_kernel,
        grid=(m // tm, n // tn),
        in_specs=[pl.BlockSpec((tm, ka), lambda i, j: (i, 0)),
                  pl.BlockSpec((tm, ka), lambda i, j: (i, 0)),
                  pl.BlockSpec((ka, tn), lambda i, j: (0, j)),
                  pl.BlockSpec((ka, tn), lambda i, j: (1, j))],
        out_specs=pl.BlockSpec((tm, tn), lambda i, j: (i, j)),
        out_shape=jax.ShapeDtypeStruct((m, n), F32),
        compiler_params=_cparams("parallel", "arbitrary"),
        name="out_proj",
    )(attn2, ssd2, w_out, w_out)


def _ln_body(x_ref, br_ref, gate_ref, g_ref, b_ref):
    v = ALPHA * x_ref[...] + gate_ref[0] * br_ref[...]
    mu = jnp.mean(v, axis=-1, keepdims=True)
    vc = v - mu
    var = jnp.mean(vc * vc, axis=-1, keepdims=True)
    return vc * lax.rsqrt(var + LN_EPS) * g_ref[...] + b_ref[...]


def _ln_mod_kernel(x_ref, br_ref, gate_ref, g_ref, b_ref, sc_ref, sh_ref, o_ref, h_ref):
    out = _ln_body(x_ref, br_ref, gate_ref, g_ref, b_ref)
    o_ref[...] = out
    h_ref[...] = (out * (1.0 + sc_ref[0]) + sh_ref[0]).astype(h_ref.dtype)


def _ln_kernel(x_ref, br_ref, gate_ref, g_ref, b_ref, o_ref):
    o_ref[...] = _ln_body(x_ref, br_ref, gate_ref, g_ref, b_ref)


def _deepnorm(x2, branch, gate, ln_g, ln_b, seq, mod=None, tm=256):
    m, d = x2.shape
    tm = min(tm, seq)
    per_batch = seq // tm
    rows = pl.BlockSpec((tm, d), lambda i: (i, 0))
    per_b = pl.BlockSpec((1, 1, d), lambda i: (i // per_batch, 0, 0))
    vec = pl.BlockSpec((1, d), lambda i: (0, 0))
    args = [x2, branch, gate, ln_g.reshape(1, d), ln_b.reshape(1, d)]
    in_specs = [rows, rows, per_b, vec, vec]
    if mod is None:
        return pl.pallas_call(
            _ln_kernel, grid=(m // tm,), in_specs=in_specs, out_specs=rows,
            out_shape=jax.ShapeDtypeStruct((m, d), F32),
            compiler_params=_cparams("parallel"), name="deepnorm_ln",
        )(*args)
    return pl.pallas_call(
        _ln_mod_kernel, grid=(m // tm,), in_specs=in_specs + [per_b, per_b],
        out_specs=[rows, rows],
        out_shape=[jax.ShapeDtypeStruct((m, d), F32), jax.ShapeDtypeStruct((m, d), BF16)],
        compiler_params=_cparams("parallel"), name="deepnorm_ln_mod",
    )(*args, *mod)


def _ffn_up_kernel(h_ref, wg_ref, wu_ref, o_ref):
    h = h_ref[...]
    gate = jnp.dot(h, wg_ref[...], preferred_element_type=F32)
    up = jnp.dot(h, wu_ref[...], preferred_element_type=F32)
    o_ref[...] = (_silu(gate) * up).astype(o_ref.dtype)


def _ffn_up(h2, w_gate, w_up, tm=1024, tn=256):
    m, d = h2.shape
    n = w_gate.shape[1]
    tm = min(tm, m)
    wspec = pl.BlockSpec((d, tn), lambda i, j: (0, j))
    return pl.pallas_call(
        _ffn_up_kernel,
        grid=(m // tm, n // tn),
        in_specs=[pl.BlockSpec((tm, d), lambda i, j: (i, 0)), wspec, wspec],
        out_specs=pl.BlockSpec((tm, tn), lambda i, j: (i, j)),
        out_shape=jax.ShapeDtypeStruct((m, n), BF16),
        compiler_params=_cparams("parallel", "arbitrary"),
        name="ffn_up",
    )(h2, w_gate, w_up)


def _mm_kernel(a_ref, w_ref, o_ref):
    o_ref[...] = jnp.dot(a_ref[...], w_ref[...], preferred_element_type=F32).astype(o_ref.dtype)


def _ffn_down(act, w_down, tm=512, tn=512):
    m, k = act.shape
    n = w_down.shape[1]
    tm = min(tm, m)
    return pl.pallas_call(
        _mm_kernel,
        grid=(m // tm, n // tn),
        in_specs=[pl.BlockSpec((tm, k), lambda i, j: (i, 0)),
                  pl.BlockSpec((k, tn), lambda i, j: (0, j))],
        out_specs=pl.BlockSpec((tm, tn), lambda i, j: (i, j)),
        out_shape=jax.ShapeDtypeStruct((m, n), F32),
        compiler_params=_cparams("parallel", "arbitrary"),
        name="ffn_down",
    )(act, w_down)


def _rope_tables(positions):
    half = ROPE_DIM // 2
    inv_freq = ROPE_THETA ** (-jnp.arange(0, ROPE_DIM, 2, dtype=F32) / ROPE_DIM)
    ang = positions.astype(F32)[..., None] * inv_freq
    cos, sin = jnp.cos(ang), jnp.sin(ang)
    rest = HEAD_DIM - ROPE_DIM
    shp = ang.shape[:-1]
    cos_t = jnp.concatenate([cos, cos, jnp.ones(shp + (rest,), F32)], axis=-1)
    s1_t = jnp.concatenate([-sin, jnp.zeros(shp + (HEAD_DIM - half,), F32)], axis=-1)
    s2_t = jnp.concatenate([jnp.zeros(shp + (half,), F32), sin, jnp.zeros(shp + (rest,), F32)], axis=-1)
    return cos_t, s1_t, s2_t


def kernel(x, c, positions, w_ada, b_ada, w_in, conv_w, conv_b, attn_sink, a_log_fwd, a_log_bwd,
           dt_bias_fwd, dt_bias_bwd, ssd_d, ssd_norm_w, attn_norm_w, w_out, ln1_g, ln1_b,
           w_gate, w_up, w_down, ln2_g, ln2_b):
    assert w_ada.shape[0] == 1, "single-layer block"
    bsz, seq, d = x.shape
    m = bsz * seq
    x2 = x.reshape(m, d)

    mod = _ada(c, w_ada[0], b_ada[0])
    sh1, sc1, g1, sh2, sc2, g2 = [t.reshape(bsz, 1, d) for t in jnp.split(mod, 6, axis=-1)]

    w_in0 = w_in[0]
    w_main = w_in0[:, :MAIN_WIDTH].astype(BF16)
    w_dt = jnp.pad(w_in0[:, MAIN_WIDTH:], ((0, 0), (0, DT_LANES - 2 * SSD_HEADS))).astype(BF16)
    proj, dt_raw = _inproj(x2, sc1, sh1, w_main, w_dt, seq)
    proj3 = proj.reshape(bsz, seq, MAIN_WIDTH)

    qr, kr = _rope(proj3, *_rope_tables(positions))
    attn = _attention(qr, kr, proj3, attn_sink[0], attn_norm_w[0])

    conv_w8 = jnp.pad(conv_w[0], ((0, 8 - CONV_WIDTH), (0, 0)))
    xbc = _conv(proj3, conv_w8, conv_b[0])
    zeros = jnp.zeros((DT_LANES - 2 * SSD_HEADS,), F32)
    par = jnp.zeros((8, LANES), F32)
    par = par.at[0].set(jnp.concatenate([dt_bias_fwd[0], dt_bias_bwd[0], zeros]))
    par = par.at[1].set(jnp.concatenate([a_log_fwd[0], a_log_bwd[0], zeros]))
    dt_raw3 = dt_raw.reshape(bsz, seq, DT_LANES)
    y_f = _ssd_scan(xbc, dt_raw3, par, reverse=False)
    y_b = _ssd_scan(xbc, dt_raw3, par, reverse=True)
    d_x = jnp.repeat(ssd_d[0], SSD_HEAD_DIM).reshape(SSD_GROUPS, 1, GROUP_WIDTH)
    ssd = _gate(y_f, y_b, xbc, proj3, d_x, ssd_norm_w[0].reshape(SSD_GROUPS, 1, GROUP_WIDTH))

    mix = _outproj(attn.reshape(m, ATTN_WIDTH), ssd.reshape(m, SSD_WIDTH), w_out[0].astype(BF16))
    x1, h2 = _deepnorm(x2, mix, g1, ln1_g[0], ln1_b[0], seq, mod=(sc2, sh2))

    act = _ffn_up(h2, w_gate[0].astype(BF16), w_up[0].astype(BF16))
    ffn = _ffn_down(act, w_down[0].astype(BF16))
    out = _deepnorm(x1, ffn, g2, ln2_g[0], ln2_b[0], seq)
    return out.reshape(bsz, seq, d)
```

```python
import functools

import jax
import jax.numpy as jnp
from jax import lax
from jax.experimental import pallas as pl
from jax.experimental.pallas import tpu as pltpu

F32 = jnp.float32
BF16 = jnp.bfloat16

D_MODEL = 4096
HEAD_DIM = 128
ATTN_WIDTH = D_MODEL // 2
N_Q_HEADS = ATTN_WIDTH // HEAD_DIM
N_KV_HEADS = N_Q_HEADS // 4
Q_PER_KV = N_Q_HEADS // N_KV_HEADS
KV_WIDTH = N_KV_HEADS * HEAD_DIM
WINDOW = 128
BLOCK = 128
ROPE_DIM = HEAD_DIM // 4
ROPE_THETA = 500000.0
SSD_WIDTH = D_MODEL // 2
SSD_HEAD_DIM = 64
SSD_HEADS = SSD_WIDTH // SSD_HEAD_DIM
SSD_GROUPS = 4
HEADS_PER_GROUP = SSD_HEADS // SSD_GROUPS
GROUP_WIDTH = SSD_WIDTH // SSD_GROUPS
D_STATE = 128
CONV_WIDTH = 5
CHUNK = 128
XBC_WIDTH = SSD_WIDTH + 2 * SSD_GROUPS * D_STATE
MAIN_WIDTH = ATTN_WIDTH + 2 * KV_WIDTH + SSD_WIDTH + XBC_WIDTH
D_FF = -(-8 * D_MODEL // (3 * 256)) * 256
LN_EPS = 1e-5
RMS_EPS = 1e-6
ALPHA = 2.0 ** 0.25

Q_OFF = 0
K_OFF = ATTN_WIDTH
V_OFF = K_OFF + KV_WIDTH
Z_OFF = V_OFF + KV_WIDTH
XBC_OFF = Z_OFF + SSD_WIDTH
assert MAIN_WIDTH == XBC_OFF + XBC_WIDTH

V7X_VMEM_BYTES = 64 * 1024 * 1024
VMEM_LIMIT = V7X_VMEM_BYTES * 7 // 8
LANES = 128
DT_LANES = LANES
HALO_ROWS = 16
DOT_ROWS = 1024


def _cparams(*sem):
    return pltpu.CompilerParams(dimension_semantics=sem, vmem_limit_bytes=VMEM_LIMIT)


def _silu(v):
    return v / (1.0 + jnp.exp(-v))


def _dot_f32(a, b):
    return jnp.dot(a, b, preferred_element_type=F32)


def _ada_kernel(ct_ref, w_ref, b_ref, o_ref):
    k = pl.program_id(1)
    nb = o_ref.shape[0]

    @pl.when(k == 0)
    def _():
        o_ref[...] = jnp.broadcast_to(b_ref[...], o_ref.shape)

    cond = _silu(ct_ref[...])
    w = w_ref[...]
    rows = [jnp.sum(w * cond[:, b:b + 1], axis=0, keepdims=True) for b in range(nb)]
    o_ref[...] += jnp.concatenate(rows, axis=0)


def _ada(c, w_ada, b_ada, tk=1024, tn=2048):
    bsz, d = c.shape
    n = w_ada.shape[1]
    return pl.pallas_call(
        _ada_kernel,
        grid=(n // tn, d // tk),
        in_specs=[pl.BlockSpec((tk, bsz), lambda j, k: (k, 0)),
                  pl.BlockSpec((tk, tn), lambda j, k: (k, j)),
                  pl.BlockSpec((1, tn), lambda j, k: (0, j))],
        out_specs=pl.BlockSpec((bsz, tn), lambda j, k: (0, j)),
        out_shape=jax.ShapeDtypeStruct((bsz, n), F32),
        compiler_params=_cparams("parallel", "arbitrary"),
        name="ada_mod",
    )(c.T, w_ada, b_ada.reshape(1, n))


def _modulate_kernel(x_ref, sc_ref, sh_ref, h_ref):
    h_ref[...] = (x_ref[...] * (1.0 + sc_ref[0]) + sh_ref[0]).astype(h_ref.dtype)


def _modulate(x2, scale, shift, seq, tm=512):
    m, d = x2.shape
    tm = min(tm, seq)
    per_batch = seq // tm
    per_b = pl.BlockSpec((1, 1, d), lambda i: (i // per_batch, 0, 0))
    return pl.pallas_call(
        _modulate_kernel,
        grid=(m // tm,),
        in_specs=[pl.BlockSpec((tm, d), lambda i: (i, 0)), per_b, per_b],
        out_specs=pl.BlockSpec((tm, d), lambda i: (i, 0)),
        out_shape=jax.ShapeDtypeStruct((m, d), BF16),
        compiler_params=_cparams("parallel"),
        name="modulate",
    )(x2, scale, shift)


def _resident(block_shape, index_map):
    return pl.BlockSpec(block_shape, index_map, pipeline_mode=pl.Buffered(1))


def _row_chunks(rows):
    step = min(rows, DOT_ROWS)
    return [slice(r, r + step) for r in range(0, rows, step)]


def _inproj_kernel(h_ref, w_ref, wdt_ref, o_ref, odt_ref):
    @pl.when(pl.program_id(1) == 0)
    def _():
        for rs in _row_chunks(h_ref.shape[0]):
            odt_ref[rs, :] = _dot_f32(h_ref[rs, :], wdt_ref[...])

    w = w_ref[...].astype(BF16)
    for rs in _row_chunks(h_ref.shape[0]):
        o_ref[rs, :] = _dot_f32(h_ref[rs, :], w).astype(o_ref.dtype)


def _inproj(h, w_in, w_dt, tm=2048, tn=256):
    m, d = h.shape
    tm = min(tm, m)
    return pl.pallas_call(
        _inproj_kernel,
        grid=(m // tm, MAIN_WIDTH // tn),
        in_specs=[_resident((tm, d), lambda i, j: (i, 0)),
                  pl.BlockSpec((d, tn), lambda i, j: (0, j)),
                  pl.BlockSpec((d, DT_LANES), lambda i, j: (0, 0))],
        out_specs=[pl.BlockSpec((tm, tn), lambda i, j: (i, j)),
                   pl.BlockSpec((tm, DT_LANES), lambda i, j: (i, 0))],
        out_shape=[jax.ShapeDtypeStruct((m, MAIN_WIDTH), BF16),
                   jax.ShapeDtypeStruct((m, DT_LANES), F32)],
        compiler_params=_cparams("parallel", "arbitrary"),
        name="in_proj",
    )(h, w_in, w_dt)


def _rope_kernel(q_ref, k_ref, c_ref, s1_ref, s2_ref, qo_ref, ko_ref):
    cos, s1, s2 = c_ref[0], s1_ref[0], s2_ref[0]
    half = ROPE_DIM // 2

    def rot(t):
        t = t.astype(F32)
        return (t * cos + pltpu.roll(t, HEAD_DIM - half, 1) * s1 + pltpu.roll(t, half, 1) * s2)

    for h in range(N_Q_HEADS):
        sl = slice(h * HEAD_DIM, (h + 1) * HEAD_DIM)
        qo_ref[0, :, sl] = rot(q_ref[0, :, sl]).astype(qo_ref.dtype)
    for h in range(N_KV_HEADS):
        sl = slice(h * HEAD_DIM, (h + 1) * HEAD_DIM)
        ko_ref[0, :, sl] = rot(k_ref[0, :, sl]).astype(ko_ref.dtype)


def _rope(proj3, cos_t, s1_t, s2_t, ts=256):
    bsz, seq, _ = proj3.shape
    ts = min(ts, seq)
    tab = pl.BlockSpec((1, ts, HEAD_DIM), lambda b, i: (b, i, 0))
    return pl.pallas_call(
        _rope_kernel,
        grid=(bsz, seq // ts),
        in_specs=[pl.BlockSpec((1, ts, ATTN_WIDTH), lambda b, i: (b, i, Q_OFF // ATTN_WIDTH)),
                  pl.BlockSpec((1, ts, KV_WIDTH), lambda b, i: (b, i, K_OFF // KV_WIDTH)),
                  tab, tab, tab],
        out_specs=[pl.BlockSpec((1, ts, ATTN_WIDTH), lambda b, i: (b, i, 0)),
                   pl.BlockSpec((1, ts, KV_WIDTH), lambda b, i: (b, i, 0))],
        out_shape=[jax.ShapeDtypeStruct((bsz, seq, ATTN_WIDTH), BF16),
                   jax.ShapeDtypeStruct((bsz, seq, KV_WIDTH), BF16)],
        compiler_params=_cparams("parallel", "parallel"),
        name="rope",
    )(proj3, proj3, cos_t, s1_t, s2_t)


def _attn_kernel(sink_ref, q_ref, kp_ref, kc_ref, kn_ref, vp_ref, vc_ref, vn_ref, nw_ref,
                 o_ref, acc_ref, *, seq):
    i = pl.program_id(1)
    qi = lax.broadcasted_iota(jnp.int32, (BLOCK, 3 * BLOCK), 0)
    sj = lax.broadcasted_iota(jnp.int32, (BLOCK, 3 * BLOCK), 1)
    kpos = i * BLOCK + sj - BLOCK
    keep = jnp.where(jnp.abs(sj - BLOCK - qi) <= WINDOW,
                     jnp.where(kpos >= 0, jnp.where(kpos < seq, 1, 0), 0), 0)
    bias = jnp.where(keep == 1, 0.0, -jnp.inf).astype(F32)
    scale = HEAD_DIM ** -0.5
    ssq = jnp.zeros((BLOCK, 1), F32)
    for kv in range(N_KV_HEADS):
        ksl = slice(kv * HEAD_DIM, (kv + 1) * HEAD_DIM)
        kwin = jnp.concatenate([kp_ref[0, :, ksl], kc_ref[0, :, ksl], kn_ref[0, :, ksl]], axis=0)
        vwin = jnp.concatenate([vp_ref[0, :, ksl], vc_ref[0, :, ksl], vn_ref[0, :, ksl]], axis=0)
        q4 = jnp.concatenate(
            [q_ref[0, :, (kv * Q_PER_KV + g) * HEAD_DIM:(kv * Q_PER_KV + g + 1) * HEAD_DIM]
             for g in range(Q_PER_KV)], axis=0)
        s = lax.dot_general(q4, kwin, (((1,), (1,)), ((), ())), preferred_element_type=F32)
        probs = []
        for g in range(Q_PER_KV):
            sink = sink_ref[kv * Q_PER_KV + g]
            sg = s[g * BLOCK:(g + 1) * BLOCK] * scale + bias
            m = jnp.maximum(jnp.max(sg, axis=-1, keepdims=True), sink)
            p = jnp.exp(sg - m)
            denom = jnp.sum(p, axis=-1, keepdims=True) + jnp.exp(sink - m)
            probs.append((p / denom).astype(BF16))
        o = jnp.dot(jnp.concatenate(probs, axis=0), vwin, preferred_element_type=F32)
        for g in range(Q_PER_KV):
            og = o[g * BLOCK:(g + 1) * BLOCK]
            h = kv * Q_PER_KV + g
            acc_ref[:, h * HEAD_DIM:(h + 1) * HEAD_DIM] = og
            ssq = ssq + jnp.sum(og * og, axis=-1, keepdims=True)
    inv = lax.rsqrt(ssq / ATTN_WIDTH + RMS_EPS)
    o_ref[0] = (acc_ref[...] * inv * nw_ref[...]).astype(o_ref.dtype)


def _attention(qr, kr, proj3, sink, norm_w):
    bsz, seq, _ = qr.shape
    nb = seq // BLOCK
    vcol = V_OFF // KV_WIDTH

    def kv_specs(col):
        return [pl.BlockSpec((1, BLOCK, KV_WIDTH), lambda b, i: (b, jnp.maximum(i - 1, 0), col)),
                pl.BlockSpec((1, BLOCK, KV_WIDTH), lambda b, i: (b, i, col)),
                pl.BlockSpec((1, BLOCK, KV_WIDTH), lambda b, i: (b, jnp.minimum(i + 1, nb - 1), col))]

    return pl.pallas_call(
        functools.partial(_attn_kernel, seq=seq),
        grid=(bsz, nb),
        in_specs=[pl.BlockSpec(memory_space=pltpu.SMEM),
                  pl.BlockSpec((1, BLOCK, ATTN_WIDTH), lambda b, i: (b, i, 0))]
                 + kv_specs(0) + kv_specs(vcol)
                 + [pl.BlockSpec((1, ATTN_WIDTH), lambda b, i: (0, 0))],
        out_specs=pl.BlockSpec((1, BLOCK, ATTN_WIDTH), lambda b, i: (b, i, 0)),
        out_shape=jax.ShapeDtypeStruct((bsz, seq, ATTN_WIDTH), BF16),
        scratch_shapes=[pltpu.VMEM((BLOCK, ATTN_WIDTH), F32)],
        compiler_params=_cparams("parallel", "parallel"),
        name="band_attn",
    )(sink, qr, kr, kr, kr, proj3, proj3, proj3, norm_w.reshape(1, ATTN_WIDTH))


def _conv_kernel(prev_ref, cur_ref, next_ref, w_ref, b_ref, o_ref):
    i = pl.program_id(1)
    last = pl.num_programs(1) - 1
    ts = cur_ref.shape[1]
    pad = CONV_WIDTH // 2
    prev = jnp.where(i > 0, prev_ref[0].astype(F32), 0.0)
    nxt = jnp.where(i < last, next_ref[0].astype(F32), 0.0)
    ext = jnp.concatenate([prev, cur_ref[0].astype(F32), nxt], axis=0)
    rows = ext.shape[0]
    w = w_ref[...]
    acc = jnp.broadcast_to(b_ref[...], (ts, ext.shape[1]))
    for j in range(CONV_WIDTH):
        d = j - pad
        shifted = ext if d == 0 else pltpu.roll(ext, (-d) % rows, 0)
        acc = acc + shifted[HALO_ROWS:HALO_ROWS + ts] * w[j:j + 1, :]
    o_ref[0] = _silu(acc).astype(o_ref.dtype)


def _conv(proj3, conv_w8, conv_b, ts=512, tc=1024):
    bsz, seq, _ = proj3.shape
    ts = min(ts, seq)
    col0 = XBC_OFF // tc
    hb = ts // HALO_ROWS
    nhalo = seq // HALO_ROWS
    return pl.pallas_call(
        _conv_kernel,
        grid=(bsz, seq // ts, XBC_WIDTH // tc),
        in_specs=[pl.BlockSpec((1, HALO_ROWS, tc), lambda b, i, j: (b, jnp.maximum(i * hb - 1, 0), col0 + j)),
                  pl.BlockSpec((1, ts, tc), lambda b, i, j: (b, i, col0 + j)),
                  pl.BlockSpec((1, HALO_ROWS, tc),
                               lambda b, i, j: (b, jnp.minimum((i + 1) * hb, nhalo - 1), col0 + j)),
                  pl.BlockSpec((8, tc), lambda b, i, j: (0, j)),
                  pl.BlockSpec((1, tc), lambda b, i, j: (0, j))],
        out_specs=pl.BlockSpec((1, ts, tc), lambda b, i, j: (b, i, j)),
        out_shape=jax.ShapeDtypeStruct((bsz, seq, XBC_WIDTH), BF16),
        compiler_params=_cparams("parallel", "parallel", "parallel"),
        name="dwconv_silu",
    )(proj3, proj3, proj3, conv_w8, conv_b.reshape(1, XBC_WIDTH))


def _bf16_pieces(v, parts):
    pieces = []
    rem = v
    for p in range(parts):
        piece = rem.astype(BF16)
        pieces.append(piece)
        if p + 1 < parts:
            rem = rem - piece.astype(F32)
    return pieces


def _ssd_chunk(xs_ref, b_ref, c_ref, dt_ref, par_ref, state_ref, emit, reverse):
    L = CHUNK

    @pl.when(pl.program_id(1) == 0)
    def _():
        state_ref[...] = jnp.zeros_like(state_ref)

    row = lax.broadcasted_iota(jnp.int32, (L, L), 0)
    col = lax.broadcasted_iota(jnp.int32, (L, L), 1)
    causal = (col >= row) if reverse else (col <= row)
    tri_b = jnp.where(causal, 1.0, 0.0).astype(BF16)

    xb = dt_ref[0] + par_ref[0:1, :]
    dt = jnp.maximum(xb, 0.0) + jnp.log1p(jnp.exp(-jnp.abs(xb)))
    adt = dt * (-jnp.exp(par_ref[1:2, :]))
    acum = sum(_dot_f32(tri_b, piece) for piece in _bf16_pieces(adt, 3))
    acum_t = acum.T
    edge = acum[0:1, :] if reverse else acum[L - 1:L, :]
    stack = jnp.concatenate([dt, jnp.exp(acum), jnp.exp(edge - acum)], axis=0)

    lane = lax.broadcasted_iota(jnp.int32, (L, LANES), 1)
    head_off = SSD_HEADS if reverse else 0
    for g in range(SSD_GROUPS):
        base = head_off + g * HEADS_PER_GROUP
        gsl = slice(g * GROUP_WIDTH, (g + 1) * GROUP_WIDTH)
        er = lax.broadcasted_iota(jnp.int32, (LANES, GROUP_WIDTH), 0)
        ec = lax.broadcasted_iota(jnp.int32, (LANES, GROUP_WIDTH), 1) // SSD_HEAD_DIM
        expand_b = jnp.where(er == ec + base, 1.0, 0.0).astype(BF16)
        ex = sum(_dot_f32(piece, expand_b) for piece in _bf16_pieces(stack, 2))
        dt_x, eac_x, ds_x = ex[0:L], ex[L:2 * L], ex[2 * L:3 * L]
        xs = xs_ref[0, :, gsl].astype(F32)
        xdt = xs * dt_x
        bm = b_ref[0, :, g * D_STATE:(g + 1) * D_STATE]
        cm = c_ref[0, :, g * D_STATE:(g + 1) * D_STATE]
        cb = lax.dot_general(cm, bm, (((1,), (1,)), ((), ())), preferred_element_type=F32)
        ys = []
        for pair in range(HEADS_PER_GROUP // 2):
            ms = []
            for r in (2 * pair, 2 * pair + 1):
                hl = base + r
                seg = acum[:, hl:hl + 1] - acum_t[hl:hl + 1, :]
                ms.append((cb * jnp.exp(jnp.where(causal, seg, -jnp.inf))).astype(BF16))
            xp = xdt[:, pair * LANES:(pair + 1) * LANES]
            rhs = jnp.concatenate([jnp.where(lane < SSD_HEAD_DIM, xp, 0.0),
                                   jnp.where(lane >= SSD_HEAD_DIM, xp, 0.0)], axis=0).astype(BF16)
            ys.append(jnp.dot(jnp.concatenate(ms, axis=1), rhs, preferred_element_type=F32))
        y_diag = jnp.concatenate(ys, axis=1)
        st = state_ref[g]
        y_off = jnp.dot(cm, st.astype(BF16), preferred_element_type=F32) * eac_x
        emit(g, gsl, xs, y_diag + y_off)
        new = lax.dot_general(bm, (xdt * ds_x).astype(BF16), (((0,), (0,)), ((), ())),
                              preferred_element_type=F32)
        cdec = eac_x[0:1, :] if reverse else eac_x[L - 1:L, :]
        state_ref[g] = st * cdec + new


def _ssd_bwd_kernel(xs_ref, b_ref, c_ref, dt_ref, par_ref, y_ref, state_ref):
    def emit(g, gsl, xs, y):
        y_ref[0, :, gsl] = y

    _ssd_chunk(xs_ref, b_ref, c_ref, dt_ref, par_ref, state_ref, emit, reverse=True)


def _ssd_fwd_gate_kernel(xs_ref, b_ref, c_ref, dt_ref, par_ref, yb_ref, zlo_ref, zhi_ref, d_ref, nw_ref,
                         o_ref, state_ref):
    half = SSD_GROUPS // 2

    def emit(g, gsl, xs, y):
        z_ref = zlo_ref if g < half else zhi_ref
        zsl = slice((g % half) * GROUP_WIDTH, (g % half + 1) * GROUP_WIDTH)
        y = y + yb_ref[0, :, gsl] + d_ref[g] * xs
        y = y * _silu(z_ref[0, :, zsl].astype(F32))
        y = y * lax.rsqrt(jnp.mean(y * y, axis=-1, keepdims=True) + RMS_EPS)
        o_ref[0, :, gsl] = (y * nw_ref[g]).astype(o_ref.dtype)

    _ssd_chunk(xs_ref, b_ref, c_ref, dt_ref, par_ref, state_ref, emit, reverse=False)


def _ssd_specs(nc, reverse):
    bc_w = SSD_GROUPS * D_STATE
    cidx = (lambda c: nc - 1 - c) if reverse else (lambda c: c)
    chunk = lambda width, col: pl.BlockSpec((1, CHUNK, width), lambda b, c: (b, cidx(c), col))
    specs = [chunk(SSD_WIDTH, 0), chunk(bc_w, SSD_WIDTH // bc_w), chunk(bc_w, SSD_WIDTH // bc_w + 1),
             chunk(DT_LANES, 0), pl.BlockSpec((8, LANES), lambda b, c: (0, 0))]
    return specs, chunk


def _ssd_bwd(xbc, dt_raw3, par):
    bsz, seq, _ = xbc.shape
    specs, chunk = _ssd_specs(seq // CHUNK, reverse=True)
    return pl.pallas_call(
        _ssd_bwd_kernel,
        grid=(bsz, seq // CHUNK),
        in_specs=specs,
        out_specs=chunk(SSD_WIDTH, 0),
        out_shape=jax.ShapeDtypeStruct((bsz, seq, SSD_WIDTH), F32),
        scratch_shapes=[pltpu.VMEM((SSD_GROUPS, D_STATE, GROUP_WIDTH), F32)],
        compiler_params=_cparams("parallel", "arbitrary"),
        name="ssd_bwd",
    )(xbc, xbc, xbc, dt_raw3, par)


def _ssd_fwd_gate(xbc, dt_raw3, par, y_b, proj3, d_x, norm_w):
    bsz, seq, _ = xbc.shape
    specs, chunk = _ssd_specs(seq // CHUNK, reverse=False)
    zw = SSD_WIDTH // 2
    per_group = pl.BlockSpec((SSD_GROUPS, 1, GROUP_WIDTH), lambda b, c: (0, 0, 0))
    return pl.pallas_call(
        _ssd_fwd_gate_kernel,
        grid=(bsz, seq // CHUNK),
        in_specs=specs + [chunk(SSD_WIDTH, 0), chunk(zw, Z_OFF // zw), chunk(zw, Z_OFF // zw + 1),
                          per_group, per_group],
        out_specs=chunk(SSD_WIDTH, 0),
        out_shape=jax.ShapeDtypeStruct((bsz, seq, SSD_WIDTH), BF16),
        scratch_shapes=[pltpu.VMEM((SSD_GROUPS, D_STATE, GROUP_WIDTH), F32)],
        compiler_params=_cparams("parallel", "arbitrary"),
        name="ssd_fwd_gate",
    )(xbc, xbc, xbc, dt_raw3, par, y_b, proj3, proj3, d_x, norm_w)


def _outproj_kernel(a_ref, s_ref, wa_ref, ws_ref, o_ref):
    wa = wa_ref[...].astype(BF16)
    ws = ws_ref[...].astype(BF16)
    for rs in _row_chunks(a_ref.shape[0]):
        o_ref[rs, :] = _dot_f32(a_ref[rs, :], wa) + _dot_f32(s_ref[rs, :], ws)


def _outproj(attn2, ssd2, w_out, tm=2048, tn=256):
    m, ka = attn2.shape
    n = w_out.shape[1]
    tm = min(tm, m)
    return pl.pallas_call(
        _outproj_kernel,
        grid=(m // tm, n // tn),
        in_specs=[_resident((tm, ka), lambda i, j: (i, 0)),
                  _resident((tm, ka), lambda i, j: (i, 0)),
                  pl.BlockSpec((ka, tn), lambda i, j: (0, j)),
                  pl.BlockSpec((ka, tn), lambda i, j: (1, j))],
        out_specs=pl.BlockSpec((tm, tn), lambda i, j: (i, j)),
        out_shape=jax.ShapeDtypeStruct((m, n), F32),
        compiler_params=_cparams("parallel", "arbitrary"),
        name="out_proj",
    )(attn2, ssd2, w_out, w_out)


def _ln_body(x_ref, br_ref, gate_ref, g_ref, b_ref):
    v = ALPHA * x_ref[...] + gate_ref[0] * br_ref[...]
    mu = jnp.mean(v, axis=-1, keepdims=True)
    vc = v - mu
    var = jnp.mean(vc * vc, axis=-1, keepdims=True)
    return vc * lax.rsqrt(var + LN_EPS) * g_ref[...] + b_ref[...]


def _ln_mod_kernel(x_ref, br_ref, gate_ref, g_ref, b_ref, sc_ref, sh_ref, o_ref, h_ref):
    out = _ln_body(x_ref, br_ref, gate_ref, g_ref, b_ref)
    o_ref[...] = out
    h_ref[...] = (out * (1.0 + sc_ref[0]) + sh_ref[0]).astype(h_ref.dtype)


def _ln_kernel(x_ref, br_ref, gate_ref, g_ref, b_ref, o_ref):
    o_ref[...] = _ln_body(x_ref, br_ref, gate_ref, g_ref, b_ref)


def _deepnorm(x2, branch, gate, ln_g, ln_b, seq, mod=None, tm=256):
    m, d = x2.shape
    tm = min(tm, seq)
    per_batch = seq // tm
    rows = pl.BlockSpec((tm, d), lambda i: (i, 0))
    per_b = pl.BlockSpec((1, 1, d), lambda i: (i // per_batch, 0, 0))
    vec = pl.BlockSpec((1, d), lambda i: (0, 0))
    args = [x2, branch, gate, ln_g.reshape(1, d), ln_b.reshape(1, d)]
    in_specs = [rows, rows, per_b, vec, vec]
    if mod is None:
        return pl.pallas_call(
            _ln_kernel, grid=(m // tm,), in_specs=in_specs, out_specs=rows,
            out_shape=jax.ShapeDtypeStruct((m, d), F32),
            compiler_params=_cparams("parallel"), name="deepnorm_ln",
        )(*args)
    return pl.pallas_call(
        _ln_mod_kernel, grid=(m // tm,), in_specs=in_specs + [per_b, per_b],
        out_specs=[rows, rows],
        out_shape=[jax.ShapeDtypeStruct((m, d), F32), jax.ShapeDtypeStruct((m, d), BF16)],
        compiler_params=_cparams("parallel"), name="deepnorm_ln_mod",
    )(*args, *mod)


def _ffn_up_kernel(h_ref, wg_ref, wu_ref, wd_ref, o_ref, wdo_ref):
    tn = wg_ref.shape[1]
    w = jnp.concatenate([wg_ref[...].astype(BF16), wu_ref[...].astype(BF16)], axis=1)
    for rs in _row_chunks(h_ref.shape[0]):
        r = _dot_f32(h_ref[rs, :], w)
        o_ref[rs, :] = (_silu(r[:, :tn]) * r[:, tn:]).astype(o_ref.dtype)
    wdo_ref[...] = wd_ref[...].astype(wdo_ref.dtype)


def _ffn_up(h2, w_gate, w_up, w_down, tm=2048, tn=256):
    m, d = h2.shape
    n = w_gate.shape[1]
    tm = min(tm, m)
    nj = n // tn
    steps = (m // tm) * nj
    slab, rem = divmod(w_down.shape[0], steps)
    assert rem == 0 and slab % HALO_ROWS == 0, (w_down.shape, steps)
    wspec = pl.BlockSpec((d, tn), lambda i, j: (0, j))
    slab_spec = pl.BlockSpec((slab, w_down.shape[1]), lambda i, j: (i * nj + j, 0))
    return pl.pallas_call(
        _ffn_up_kernel,
        grid=(m // tm, nj),
        in_specs=[_resident((tm, d), lambda i, j: (i, 0)), wspec, wspec, slab_spec],
        out_specs=[pl.BlockSpec((tm, tn), lambda i, j: (i, j)), slab_spec],
        out_shape=[jax.ShapeDtypeStruct((m, n), BF16),
                   jax.ShapeDtypeStruct(w_down.shape, BF16)],
        compiler_params=_cparams("parallel", "arbitrary"),
        name="ffn_up",
    )(h2, w_gate, w_up, w_down)


def _mm_kernel(a_ref, w_ref, o_ref):
    o_ref[...] = jnp.dot(a_ref[...], w_ref[...], preferred_element_type=F32).astype(o_ref.dtype)


def _ffn_down(act, w_down, tm=512, tn=512):
    m, k = act.shape
    n = w_down.shape[1]
    tm = min(tm, m)
    return pl.pallas_call(
        _mm_kernel,
        grid=(m // tm, n // tn),
        in_specs=[pl.BlockSpec((tm, k), lambda i, j: (i, 0)),
                  pl.BlockSpec((k, tn), lambda i, j: (0, j))],
        out_specs=pl.BlockSpec((tm, tn), lambda i, j: (i, j)),
        out_shape=jax.ShapeDtypeStruct((m, n), F32),
        compiler_params=_cparams("parallel", "arbitrary"),
        name="ffn_down",
    )(act, w_down)


def _rope_tables(positions):
    half = ROPE_DIM // 2
    inv_freq = ROPE_THETA ** (-jnp.arange(0, ROPE_DIM, 2, dtype=F32) / ROPE_DIM)
    ang = positions.astype(F32)[..., None] * inv_freq
    cos, sin = jnp.cos(ang), jnp.sin(ang)
    rest = HEAD_DIM - ROPE_DIM
    shp = ang.shape[:-1]
    cos_t = jnp.concatenate([cos, cos, jnp.ones(shp + (rest,), F32)], axis=-1)
    s1_t = jnp.concatenate([-sin, jnp.zeros(shp + (HEAD_DIM - half,), F32)], axis=-1)
    s2_t = jnp.concatenate([jnp.zeros(shp + (half,), F32), sin, jnp.zeros(shp + (rest,), F32)], axis=-1)
    return cos_t, s1_t, s2_t


def kernel(x, c, positions, w_ada, b_ada, w_in, conv_w, conv_b, attn_sink, a_log_fwd, a_log_bwd,
           dt_bias_fwd, dt_bias_bwd, ssd_d, ssd_norm_w, attn_norm_w, w_out, ln1_g, ln1_b,
           w_gate, w_up, w_down, ln2_g, ln2_b):
    assert w_ada.shape[0] == 1, "single-layer block"
    bsz, seq, d = x.shape
    m = bsz * seq
    x2 = x.reshape(m, d)

    mod = _ada(c, w_ada[0], b_ada[0])
    sh1, sc1, g1, sh2, sc2, g2 = [t.reshape(bsz, 1, d) for t in jnp.split(mod, 6, axis=-1)]

    w_in0 = w_in[0]
    w_dt = jnp.pad(w_in0[:, MAIN_WIDTH:], ((0, 0), (0, DT_LANES - 2 * SSD_HEADS))).astype(BF16)
    proj, dt_raw = _inproj(_modulate(x2, sc1, sh1, seq), w_in0, w_dt)
    proj3 = proj.reshape(bsz, seq, MAIN_WIDTH)

    qr, kr = _rope(proj3, *_rope_tables(positions))
    attn = _attention(qr, kr, proj3, attn_sink[0], attn_norm_w[0])

    conv_w8 = jnp.pad(conv_w[0], ((0, 8 - CONV_WIDTH), (0, 0)))
    xbc = _conv(proj3, conv_w8, conv_b[0])
    zeros = jnp.zeros((DT_LANES - 2 * SSD_HEADS,), F32)
    par = jnp.zeros((8, LANES), F32)
    par = par.at[0].set(jnp.concatenate([dt_bias_fwd[0], dt_bias_bwd[0], zeros]))
    par = par.at[1].set(jnp.concatenate([a_log_fwd[0], a_log_bwd[0], zeros]))
    dt_raw3 = dt_raw.reshape(bsz, seq, DT_LANES)
    y_b = _ssd_bwd(xbc, dt_raw3, par)
    d_x = jnp.repeat(ssd_d[0], SSD_HEAD_DIM).reshape(SSD_GROUPS, 1, GROUP_WIDTH)
    ssd = _ssd_fwd_gate(xbc, dt_raw3, par, y_b, proj3, d_x,
                        ssd_norm_w[0].reshape(SSD_GROUPS, 1, GROUP_WIDTH))

    mix = _outproj(attn.reshape(m, ATTN_WIDTH), ssd.reshape(m, SSD_WIDTH), w_out[0])
    x1, h2 = _deepnorm(x2, mix, g1, ln1_g[0], ln1_b[0], seq, mod=(sc2, sh2))

    act, w_down_b = _ffn_up(h2, w_gate[0], w_up[0], w_down[0])
    ffn = _ffn_down(act, w_down_b)
    out = _deepnorm(x1, ffn, g2, ln2_g[0], ln2_b[0], seq)
    return out.reshape(bsz, seq, d)
```

```python
import functools

import jax
import jax.numpy as jnp
from jax import lax
from jax.experimental import pallas as pl
from jax.experimental.pallas import tpu as pltpu

F32 = jnp.float32
BF16 = jnp.bfloat16

D_MODEL = 4096
HEAD_DIM = 128
ATTN_WIDTH = D_MODEL // 2
N_Q_HEADS = ATTN_WIDTH // HEAD_DIM
N_KV_HEADS = N_Q_HEADS // 4
Q_PER_KV = N_Q_HEADS // N_KV_HEADS
KV_WIDTH = N_KV_HEADS * HEAD_DIM
WINDOW = 128
BLOCK = 128
ROPE_DIM = HEAD_DIM // 4
ROPE_THETA = 500000.0
SSD_WIDTH = D_MODEL // 2
SSD_HEAD_DIM = 64
SSD_HEADS = SSD_WIDTH // SSD_HEAD_DIM
SSD_GROUPS = 4
HEADS_PER_GROUP = SSD_HEADS // SSD_GROUPS
GROUP_WIDTH = SSD_WIDTH // SSD_GROUPS
D_STATE = 128
CONV_WIDTH = 5
CHUNK = 128
XBC_WIDTH = SSD_WIDTH + 2 * SSD_GROUPS * D_STATE
MAIN_WIDTH = ATTN_WIDTH + 2 * KV_WIDTH + SSD_WIDTH + XBC_WIDTH
D_FF = -(-8 * D_MODEL // (3 * 256)) * 256
LN_EPS = 1e-5
RMS_EPS = 1e-6
ALPHA = 2.0 ** 0.25

Q_OFF = 0
K_OFF = ATTN_WIDTH
V_OFF = K_OFF + KV_WIDTH
Z_OFF = V_OFF + KV_WIDTH
XBC_OFF = Z_OFF + SSD_WIDTH
assert MAIN_WIDTH == XBC_OFF + XBC_WIDTH

V7X_VMEM_BYTES = 64 * 1024 * 1024
VMEM_LIMIT = V7X_VMEM_BYTES * 7 // 8
LANES = 128
DT_LANES = LANES
HALO_ROWS = 16
DOT_ROWS = 1024


def _cparams(*sem):
    return pltpu.CompilerParams(dimension_semantics=sem, vmem_limit_bytes=VMEM_LIMIT)


def _silu(v):
    return v / (1.0 + jnp.exp(-v))


def _dot_f32(a, b):
    return jnp.dot(a, b, preferred_element_type=F32)


def _ada_kernel(ct_ref, w_ref, b_ref, o_ref):
    k = pl.program_id(1)
    nb = o_ref.shape[0]

    @pl.when(k == 0)
    def _():
        o_ref[...] = jnp.broadcast_to(b_ref[...], o_ref.shape)

    cond = _silu(ct_ref[...])
    w = w_ref[...]
    rows = [jnp.sum(w * cond[:, b:b + 1], axis=0, keepdims=True) for b in range(nb)]
    o_ref[...] += jnp.concatenate(rows, axis=0)


def _ada(c, w_ada, b_ada, tk=1024, tn=2048):
    bsz, d = c.shape
    n = w_ada.shape[1]
    return pl.pallas_call(
        _ada_kernel,
        grid=(n // tn, d // tk),
        in_specs=[pl.BlockSpec((tk, bsz), lambda j, k: (k, 0)),
                  pl.BlockSpec((tk, tn), lambda j, k: (k, j)),
                  pl.BlockSpec((1, tn), lambda j, k: (0, j))],
        out_specs=pl.BlockSpec((bsz, tn), lambda j, k: (0, j)),
        out_shape=jax.ShapeDtypeStruct((bsz, n), F32),
        compiler_params=_cparams("parallel", "arbitrary"),
        name="ada_mod",
    )(c.T, w_ada, b_ada.reshape(1, n))


def _modulate_kernel(x_ref, sc_ref, sh_ref, h_ref):
    h_ref[...] = (x_ref[...] * (1.0 + sc_ref[0]) + sh_ref[0]).astype(h_ref.dtype)


def _modulate(x2, scale, shift, seq, tm=512):
    m, d = x2.shape
    tm = min(tm, seq)
    per_batch = seq // tm
    per_b = pl.BlockSpec((1, 1, d), lambda i: (i // per_batch, 0, 0))
    return pl.pallas_call(
        _modulate_kernel,
        grid=(m // tm,),
        in_specs=[pl.BlockSpec((tm, d), lambda i: (i, 0)), per_b, per_b],
        out_specs=pl.BlockSpec((tm, d), lambda i: (i, 0)),
        out_shape=jax.ShapeDtypeStruct((m, d), BF16),
        compiler_params=_cparams("parallel"),
        name="modulate",
    )(x2, scale, shift)


def _resident(block_shape, index_map):
    return pl.BlockSpec(block_shape, index_map, pipeline_mode=pl.Buffered(1))


def _row_chunks(rows):
    step = min(rows, DOT_ROWS)
    return [slice(r, r + step) for r in range(0, rows, step)]


def _dot_nt(a, bt):
    return lax.dot_general(a, bt, (((1,), (1,)), ((), ())), preferred_element_type=F32)


def _inproj_kernel(h_ref, wt_ref, wdt_ref, o_ref, odt_ref):
    @pl.when(pl.program_id(1) == 0)
    def _():
        wdt = wdt_ref[...].astype(BF16)
        for rs in _row_chunks(h_ref.shape[0]):
            r = _dot_nt(h_ref[rs, :], wdt)
            odt_ref[rs, :] = jnp.concatenate([r, jnp.zeros((r.shape[0], DT_LANES - r.shape[1]), F32)], axis=1)

    wt = wt_ref[...].astype(BF16)
    for rs in _row_chunks(h_ref.shape[0]):
        o_ref[rs, :] = _dot_nt(h_ref[rs, :], wt).astype(o_ref.dtype)


def _inproj(h, w_in_t, tm=2048, tn=256):
    m, d = h.shape
    tm = min(tm, m)
    n_dt = w_in_t.shape[0] - MAIN_WIDTH
    assert MAIN_WIDTH % n_dt == 0 and n_dt % 8 == 0
    return pl.pallas_call(
        _inproj_kernel,
        grid=(m // tm, MAIN_WIDTH // tn),
        in_specs=[pl.BlockSpec((tm, d), lambda i, j: (i, 0)),
                  pl.BlockSpec((tn, d), lambda i, j: (j, 0)),
                  pl.BlockSpec((n_dt, d), lambda i, j: (MAIN_WIDTH // n_dt, 0))],
        out_specs=[pl.BlockSpec((tm, tn), lambda i, j: (i, j)),
                   pl.BlockSpec((tm, DT_LANES), lambda i, j: (i, 0))],
        out_shape=[jax.ShapeDtypeStruct((m, MAIN_WIDTH), BF16),
                   jax.ShapeDtypeStruct((m, DT_LANES), F32)],
        compiler_params=_cparams("parallel", "arbitrary"),
        name="in_proj",
    )(h, w_in_t, w_in_t)


def _rope_kernel(q_ref, k_ref, c_ref, s1_ref, s2_ref, qo_ref, ko_ref):
    cos, s1, s2 = c_ref[0], s1_ref[0], s2_ref[0]
    half = ROPE_DIM // 2

    def rot(t):
        t = t.astype(F32)
        return (t * cos + pltpu.roll(t, HEAD_DIM - half, 1) * s1 + pltpu.roll(t, half, 1) * s2)

    for h in range(N_Q_HEADS):
        sl = slice(h * HEAD_DIM, (h + 1) * HEAD_DIM)
        qo_ref[0, :, sl] = rot(q_ref[0, :, sl]).astype(qo_ref.dtype)
    for h in range(N_KV_HEADS):
        sl = slice(h * HEAD_DIM, (h + 1) * HEAD_DIM)
        ko_ref[0, :, sl] = rot(k_ref[0, :, sl]).astype(ko_ref.dtype)


def _rope(proj3, cos_t, s1_t, s2_t, ts=256):
    bsz, seq, _ = proj3.shape
    ts = min(ts, seq)
    tab = pl.BlockSpec((1, ts, HEAD_DIM), lambda b, i: (b, i, 0))
    return pl.pallas_call(
        _rope_kernel,
        grid=(bsz, seq // ts),
        in_specs=[pl.BlockSpec((1, ts, ATTN_WIDTH), lambda b, i: (b, i, Q_OFF // ATTN_WIDTH)),
                  pl.BlockSpec((1, ts, KV_WIDTH), lambda b, i: (b, i, K_OFF // KV_WIDTH)),
                  tab, tab, tab],
        out_specs=[pl.BlockSpec((1, ts, ATTN_WIDTH), lambda b, i: (b, i, 0)),
                   pl.BlockSpec((1, ts, KV_WIDTH), lambda b, i: (b, i, 0))],
        out_shape=[jax.ShapeDtypeStruct((bsz, seq, ATTN_WIDTH), BF16),
                   jax.ShapeDtypeStruct((bsz, seq, KV_WIDTH), BF16)],
        compiler_params=_cparams("parallel", "parallel"),
        name="rope",
    )(proj3, proj3, cos_t, s1_t, s2_t)


def _attn_kernel(sink_ref, q_ref, kp_ref, kc_ref, kn_ref, vp_ref, vc_ref, vn_ref, nw_ref,
                 o_ref, acc_ref, *, seq):
    i = pl.program_id(1)
    qi = lax.broadcasted_iota(jnp.int32, (BLOCK, 3 * BLOCK), 0)
    sj = lax.broadcasted_iota(jnp.int32, (BLOCK, 3 * BLOCK), 1)
    kpos = i * BLOCK + sj - BLOCK
    keep = jnp.where(jnp.abs(sj - BLOCK - qi) <= WINDOW,
                     jnp.where(kpos >= 0, jnp.where(kpos < seq, 1, 0), 0), 0)
    bias = jnp.where(keep == 1, 0.0, -jnp.inf).astype(F32)
    scale = HEAD_DIM ** -0.5
    ssq = jnp.zeros((BLOCK, 1), F32)
    for kv in range(N_KV_HEADS):
        ksl = slice(kv * HEAD_DIM, (kv + 1) * HEAD_DIM)
        kwin = jnp.concatenate([kp_ref[0, :, ksl], kc_ref[0, :, ksl], kn_ref[0, :, ksl]], axis=0)
        vwin = jnp.concatenate([vp_ref[0, :, ksl], vc_ref[0, :, ksl], vn_ref[0, :, ksl]], axis=0)
        q4 = jnp.concatenate(
            [q_ref[0, :, (kv * Q_PER_KV + g) * HEAD_DIM:(kv * Q_PER_KV + g + 1) * HEAD_DIM]
             for g in range(Q_PER_KV)], axis=0)
        s = lax.dot_general(q4, kwin, (((1,), (1,)), ((), ())), preferred_element_type=F32)
        probs = []
        for g in range(Q_PER_KV):
            sink = sink_ref[kv * Q_PER_KV + g]
            sg = s[g * BLOCK:(g + 1) * BLOCK] * scale + bias
            m = jnp.maximum(jnp.max(sg, axis=-1, keepdims=True), sink)
            p = jnp.exp(sg - m)
            denom = jnp.sum(p, axis=-1, keepdims=True) + jnp.exp(sink - m)
            probs.append((p / denom).astype(BF16))
        o = jnp.dot(jnp.concatenate(probs, axis=0), vwin, preferred_element_type=F32)
        for g in range(Q_PER_KV):
            og = o[g * BLOCK:(g + 1) * BLOCK]
            h = kv * Q_PER_KV + g
            acc_ref[:, h * HEAD_DIM:(h + 1) * HEAD_DIM] = og
            ssq = ssq + jnp.sum(og * og, axis=-1, keepdims=True)
    inv = lax.rsqrt(ssq / ATTN_WIDTH + RMS_EPS)
    o_ref[0] = (acc_ref[...] * inv * nw_ref[...]).astype(o_ref.dtype)


def _attention(qr, kr, proj3, sink, norm_w):
    bsz, seq, _ = qr.shape
    nb = seq // BLOCK
    vcol = V_OFF // KV_WIDTH

    def kv_specs(col):
        return [pl.BlockSpec((1, BLOCK, KV_WIDTH), lambda b, i: (b, jnp.maximum(i - 1, 0), col)),
                pl.BlockSpec((1, BLOCK, KV_WIDTH), lambda b, i: (b, i, col)),
                pl.BlockSpec((1, BLOCK, KV_WIDTH), lambda b, i: (b, jnp.minimum(i + 1, nb - 1), col))]

    return pl.pallas_call(
        functools.partial(_attn_kernel, seq=seq),
        grid=(bsz, nb),
        in_specs=[pl.BlockSpec(memory_space=pltpu.SMEM),
                  pl.BlockSpec((1, BLOCK, ATTN_WIDTH), lambda b, i: (b, i, 0))]
                 + kv_specs(0) + kv_specs(vcol)
                 + [pl.BlockSpec((1, ATTN_WIDTH), lambda b, i: (0, 0))],
        out_specs=pl.BlockSpec((1, BLOCK, ATTN_WIDTH), lambda b, i: (b, i, 0)),
        out_shape=jax.ShapeDtypeStruct((bsz, seq, ATTN_WIDTH), BF16),
        scratch_shapes=[pltpu.VMEM((BLOCK, ATTN_WIDTH), F32)],
        compiler_params=_cparams("parallel", "parallel"),
        name="band_attn",
    )(sink, qr, kr, kr, kr, proj3, proj3, proj3, norm_w.reshape(1, ATTN_WIDTH))


def _conv_kernel(prev_ref, cur_ref, next_ref, w_ref, b_ref, o_ref):
    i = pl.program_id(1)
    last = pl.num_programs(1) - 1
    ts = cur_ref.shape[1]
    pad = CONV_WIDTH // 2
    prev = jnp.where(i > 0, prev_ref[0].astype(F32), 0.0)
    nxt = jnp.where(i < last, next_ref[0].astype(F32), 0.0)
    ext = jnp.concatenate([prev, cur_ref[0].astype(F32), nxt], axis=0)
    rows = ext.shape[0]
    w = w_ref[...]
    acc = jnp.broadcast_to(b_ref[...], (ts, ext.shape[1]))
    for j in range(CONV_WIDTH):
        d = j - pad
        shifted = ext if d == 0 else pltpu.roll(ext, (-d) % rows, 0)
        acc = acc + shifted[HALO_ROWS:HALO_ROWS + ts] * w[j:j + 1, :]
    o_ref[0] = _silu(acc).astype(o_ref.dtype)


def _conv(proj3, conv_w8, conv_b, ts=512, tc=1024):
    bsz, seq, _ = proj3.shape
    ts = min(ts, seq)
    col0 = XBC_OFF // tc
    hb = ts // HALO_ROWS
    nhalo = seq // HALO_ROWS
    return pl.pallas_call(
        _conv_kernel,
        grid=(bsz, seq // ts, XBC_WIDTH // tc),
        in_specs=[pl.BlockSpec((1, HALO_ROWS, tc), lambda b, i, j: (b, jnp.maximum(i * hb - 1, 0), col0 + j)),
                  pl.BlockSpec((1, ts, tc), lambda b, i, j: (b, i, col0 + j)),
                  pl.BlockSpec((1, HALO_ROWS, tc),
                               lambda b, i, j: (b, jnp.minimum((i + 1) * hb, nhalo - 1), col0 + j)),
                  pl.BlockSpec((8, tc), lambda b, i, j: (0, j)),
                  pl.BlockSpec((1, tc), lambda b, i, j: (0, j))],
        out_specs=pl.BlockSpec((1, ts, tc), lambda b, i, j: (b, i, j)),
        out_shape=jax.ShapeDtypeStruct((bsz, seq, XBC_WIDTH), BF16),
        compiler_params=_cparams("parallel", "parallel", "parallel"),
        name="dwconv_silu",
    )(proj3, proj3, proj3, conv_w8, conv_b.reshape(1, XBC_WIDTH))


def _bf16_pieces(v, parts):
    pieces = []
    rem = v
    for p in range(parts):
        piece = rem.astype(BF16)
        pieces.append(piece)
        if p + 1 < parts:
            rem = rem - piece.astype(F32)
    return pieces


def _ssd_chunk(xs_ref, b_ref, c_ref, dt_ref, par_ref, state_ref, emit, reverse):
    L = CHUNK

    @pl.when(pl.program_id(1) == 0)
    def _():
        state_ref[...] = jnp.zeros_like(state_ref)

    row = lax.broadcasted_iota(jnp.int32, (L, L), 0)
    col = lax.broadcasted_iota(jnp.int32, (L, L), 1)
    causal = (col >= row) if reverse else (col <= row)
    tri_b = jnp.where(causal, 1.0, 0.0).astype(BF16)

    xb = dt_ref[0] + par_ref[0:1, :]
    dt = jnp.maximum(xb, 0.0) + jnp.log1p(jnp.exp(-jnp.abs(xb)))
    adt = dt * (-jnp.exp(par_ref[1:2, :]))
    acum = sum(_dot_f32(tri_b, piece) for piece in _bf16_pieces(adt, 3))
    acum_t = acum.T
    edge = acum[0:1, :] if reverse else acum[L - 1:L, :]
    stack = jnp.concatenate([dt, jnp.exp(acum), jnp.exp(edge - acum)], axis=0)

    lane = lax.broadcasted_iota(jnp.int32, (L, LANES), 1)
    head_off = SSD_HEADS if reverse else 0
    for g in range(SSD_GROUPS):
        base = head_off + g * HEADS_PER_GROUP
        gsl = slice(g * GROUP_WIDTH, (g + 1) * GROUP_WIDTH)
        er = lax.broadcasted_iota(jnp.int32, (LANES, GROUP_WIDTH), 0)
        ec = lax.broadcasted_iota(jnp.int32, (LANES, GROUP_WIDTH), 1) // SSD_HEAD_DIM
        expand_b = jnp.where(er == ec + base, 1.0, 0.0).astype(BF16)
        ex = sum(_dot_f32(piece, expand_b) for piece in _bf16_pieces(stack, 2))
        dt_x, eac_x, ds_x = ex[0:L], ex[L:2 * L], ex[2 * L:3 * L]
        xs = xs_ref[0, :, gsl].astype(F32)
        xdt = xs * dt_x
        bm = b_ref[0, :, g * D_STATE:(g + 1) * D_STATE]
        cm = c_ref[0, :, g * D_STATE:(g + 1) * D_STATE]
        cb = lax.dot_general(cm, bm, (((1,), (1,)), ((), ())), preferred_element_type=F32)
        ys = []
        for pair in range(HEADS_PER_GROUP // 2):
            ms = []
            for r in (2 * pair, 2 * pair + 1):
                hl = base + r
                seg = acum[:, hl:hl + 1] - acum_t[hl:hl + 1, :]
                ms.append((cb * jnp.exp(jnp.where(causal, seg, -jnp.inf))).astype(BF16))
            xp = xdt[:, pair * LANES:(pair + 1) * LANES]
            rhs = jnp.concatenate([jnp.where(lane < SSD_HEAD_DIM, xp, 0.0),
                                   jnp.where(lane >= SSD_HEAD_DIM, xp, 0.0)], axis=0).astype(BF16)
            ys.append(jnp.dot(jnp.concatenate(ms, axis=1), rhs, preferred_element_type=F32))
        y_diag = jnp.concatenate(ys, axis=1)
        st = state_ref[g]
        y_off = jnp.dot(cm, st.astype(BF16), preferred_element_type=F32) * eac_x
        emit(g, gsl, xs, y_diag + y_off)
        new = lax.dot_general(bm, (xdt * ds_x).astype(BF16), (((0,), (0,)), ((), ())),
                              preferred_element_type=F32)
        cdec = eac_x[0:1, :] if reverse else eac_x[L - 1:L, :]
        state_ref[g] = st * cdec + new


def _ssd_bwd_kernel(xs_ref, b_ref, c_ref, dt_ref, par_ref, y_ref, state_ref):
    def emit(g, gsl, xs, y):
        y_ref[0, :, gsl] = y

    _ssd_chunk(xs_ref, b_ref, c_ref, dt_ref, par_ref, state_ref, emit, reverse=True)


def _ssd_fwd_gate_kernel(xs_ref, b_ref, c_ref, dt_ref, par_ref, yb_ref, zlo_ref, zhi_ref, d_ref, nw_ref,
                         o_ref, state_ref):
    half = SSD_GROUPS // 2

    def emit(g, gsl, xs, y):
        z_ref = zlo_ref if g < half else zhi_ref
        zsl = slice((g % half) * GROUP_WIDTH, (g % half + 1) * GROUP_WIDTH)
        y = y + yb_ref[0, :, gsl] + d_ref[g] * xs
        y = y * _silu(z_ref[0, :, zsl].astype(F32))
        y = y * lax.rsqrt(jnp.mean(y * y, axis=-1, keepdims=True) + RMS_EPS)
        o_ref[0, :, gsl] = (y * nw_ref[g]).astype(o_ref.dtype)

    _ssd_chunk(xs_ref, b_ref, c_ref, dt_ref, par_ref, state_ref, emit, reverse=False)


def _ssd_specs(nc, reverse):
    bc_w = SSD_GROUPS * D_STATE
    cidx = (lambda c: nc - 1 - c) if reverse else (lambda c: c)
    chunk = lambda width, col: pl.BlockSpec((1, CHUNK, width), lambda b, c: (b, cidx(c), col))
    specs = [chunk(SSD_WIDTH, 0), chunk(bc_w, SSD_WIDTH // bc_w), chunk(bc_w, SSD_WIDTH // bc_w + 1),
             chunk(DT_LANES, 0), pl.BlockSpec((8, LANES), lambda b, c: (0, 0))]
    return specs, chunk


def _ssd_bwd(xbc, dt_raw3, par):
    bsz, seq, _ = xbc.shape
    specs, chunk = _ssd_specs(seq // CHUNK, reverse=True)
    return pl.pallas_call(
        _ssd_bwd_kernel,
        grid=(bsz, seq // CHUNK),
        in_specs=specs,
        out_specs=chunk(SSD_WIDTH, 0),
        out_shape=jax.ShapeDtypeStruct((bsz, seq, SSD_WIDTH), F32),
        scratch_shapes=[pltpu.VMEM((SSD_GROUPS, D_STATE, GROUP_WIDTH), F32)],
        compiler_params=_cparams("parallel", "arbitrary"),
        name="ssd_bwd",
    )(xbc, xbc, xbc, dt_raw3, par)


def _ssd_fwd_gate(xbc, dt_raw3, par, y_b, proj3, d_x, norm_w):
    bsz, seq, _ = xbc.shape
    specs, chunk = _ssd_specs(seq // CHUNK, reverse=False)
    zw = SSD_WIDTH // 2
    per_group = pl.BlockSpec((SSD_GROUPS, 1, GROUP_WIDTH), lambda b, c: (0, 0, 0))
    return pl.pallas_call(
        _ssd_fwd_gate_kernel,
        grid=(bsz, seq // CHUNK),
        in_specs=specs + [chunk(SSD_WIDTH, 0), chunk(zw, Z_OFF // zw), chunk(zw, Z_OFF // zw + 1),
                          per_group, per_group],
        out_specs=chunk(SSD_WIDTH, 0),
        out_shape=jax.ShapeDtypeStruct((bsz, seq, SSD_WIDTH), BF16),
        scratch_shapes=[pltpu.VMEM((SSD_GROUPS, D_STATE, GROUP_WIDTH), F32)],
        compiler_params=_cparams("parallel", "arbitrary"),
        name="ssd_fwd_gate",
    )(xbc, xbc, xbc, dt_raw3, par, y_b, proj3, proj3, d_x, norm_w)


def _outproj_kernel(a_ref, s_ref, wa_ref, ws_ref, o_ref):
    wa = wa_ref[...].astype(BF16)
    ws = ws_ref[...].astype(BF16)
    for rs in _row_chunks(a_ref.shape[0]):
        o_ref[rs, :] = _dot_f32(a_ref[rs, :], wa) + _dot_f32(s_ref[rs, :], ws)


def _outproj(attn2, ssd2, w_out, tm=2048, tn=256):
    m, ka = attn2.shape
    n = w_out.shape[1]
    tm = min(tm, m)
    return pl.pallas_call(
        _outproj_kernel,
        grid=(m // tm, n // tn),
        in_specs=[pl.BlockSpec((tm, ka), lambda i, j: (i, 0)),
                  pl.BlockSpec((tm, ka), lambda i, j: (i, 0)),
                  pl.BlockSpec((ka, tn), lambda i, j: (0, j)),
                  pl.BlockSpec((ka, tn), lambda i, j: (1, j))],
        out_specs=pl.BlockSpec((tm, tn), lambda i, j: (i, j)),
        out_shape=jax.ShapeDtypeStruct((m, n), F32),
        compiler_params=_cparams("parallel", "arbitrary"),
        name="out_proj",
    )(attn2, ssd2, w_out, w_out)


def _ln_body(x_ref, br_ref, gate_ref, g_ref, b_ref):
    v = ALPHA * x_ref[...] + gate_ref[0] * br_ref[...]
    mu = jnp.mean(v, axis=-1, keepdims=True)
    vc = v - mu
    var = jnp.mean(vc * vc, axis=-1, keepdims=True)
    return vc * lax.rsqrt(var + LN_EPS) * g_ref[...] + b_ref[...]


def _ln_mod_kernel(x_ref, br_ref, gate_ref, g_ref, b_ref, sc_ref, sh_ref, o_ref, h_ref):
    out = _ln_body(x_ref, br_ref, gate_ref, g_ref, b_ref)
    o_ref[...] = out
    h_ref[...] = (out * (1.0 + sc_ref[0]) + sh_ref[0]).astype(h_ref.dtype)


def _ln_kernel(x_ref, br_ref, gate_ref, g_ref, b_ref, o_ref):
    o_ref[...] = _ln_body(x_ref, br_ref, gate_ref, g_ref, b_ref)


def _deepnorm(x2, branch, gate, ln_g, ln_b, seq, mod=None, tm=256):
    m, d = x2.shape
    tm = min(tm, seq)
    per_batch = seq // tm
    rows = pl.BlockSpec((tm, d), lambda i: (i, 0))
    per_b = pl.BlockSpec((1, 1, d), lambda i: (i // per_batch, 0, 0))
    vec = pl.BlockSpec((1, d), lambda i: (0, 0))
    args = [x2, branch, gate, ln_g.reshape(1, d), ln_b.reshape(1, d)]
    in_specs = [rows, rows, per_b, vec, vec]
    if mod is None:
        return pl.pallas_call(
            _ln_kernel, grid=(m // tm,), in_specs=in_specs, out_specs=rows,
            out_shape=jax.ShapeDtypeStruct((m, d), F32),
            compiler_params=_cparams("parallel"), name="deepnorm_ln",
        )(*args)
    return pl.pallas_call(
        _ln_mod_kernel, grid=(m // tm,), in_specs=in_specs + [per_b, per_b],
        out_specs=[rows, rows],
        out_shape=[jax.ShapeDtypeStruct((m, d), F32), jax.ShapeDtypeStruct((m, d), BF16)],
        compiler_params=_cparams("parallel"), name="deepnorm_ln_mod",
    )(*args, *mod)


def _ffn_up_kernel(h_ref, wg_ref, wu_ref, wd_ref, o_ref, wdo_ref):
    tn = wg_ref.shape[1]
    w = jnp.concatenate([wg_ref[...].astype(BF16), wu_ref[...].astype(BF16)], axis=1)
    for rs in _row_chunks(h_ref.shape[0]):
        r = _dot_f32(h_ref[rs, :], w)
        o_ref[rs, :] = (_silu(r[:, :tn]) * r[:, tn:]).astype(o_ref.dtype)
    wdo_ref[...] = wd_ref[...].astype(wdo_ref.dtype)


def _ffn_up(h2, w_gate, w_up, w_down, tm=2048, tn=256):
    m, d = h2.shape
    n = w_gate.shape[1]
    tm = min(tm, m)
    nj = n // tn
    steps = (m // tm) * nj
    slab, rem = divmod(w_down.shape[0], steps)
    assert rem == 0 and slab % HALO_ROWS == 0, (w_down.shape, steps)
    wspec = pl.BlockSpec((d, tn), lambda i, j: (0, j))
    slab_spec = pl.BlockSpec((slab, w_down.shape[1]), lambda i, j: (i * nj + j, 0))
    return pl.pallas_call(
        _ffn_up_kernel,
        grid=(m // tm, nj),
        in_specs=[_resident((tm, d), lambda i, j: (i, 0)), wspec, wspec, slab_spec],
        out_specs=[pl.BlockSpec((tm, tn), lambda i, j: (i, j)), slab_spec],
        out_shape=[jax.ShapeDtypeStruct((m, n), BF16),
                   jax.ShapeDtypeStruct(w_down.shape, BF16)],
        compiler_params=_cparams("parallel", "arbitrary"),
        name="ffn_up",
    )(h2, w_gate, w_up, w_down)


def _mm_kernel(a_ref, w_ref, o_ref):
    o_ref[...] = jnp.dot(a_ref[...], w_ref[...], preferred_element_type=F32).astype(o_ref.dtype)


def _ffn_down(act, w_down, tm=512, tn=512):
    m, k = act.shape
    n = w_down.shape[1]
    tm = min(tm, m)
    return pl.pallas_call(
        _mm_kernel,
        grid=(m // tm, n // tn),
        in_specs=[pl.BlockSpec((tm, k), lambda i, j: (i, 0)),
                  pl.BlockSpec((k, tn), lambda i, j: (0, j))],
        out_specs=pl.BlockSpec((tm, tn), lambda i, j: (i, j)),
        out_shape=jax.ShapeDtypeStruct((m, n), F32),
        compiler_params=_cparams("parallel", "arbitrary"),
        name="ffn_down",
    )(act, w_down)


def _rope_tables(positions):
    half = ROPE_DIM // 2
    inv_freq = ROPE_THETA ** (-jnp.arange(0, ROPE_DIM, 2, dtype=F32) / ROPE_DIM)
    ang = positions.astype(F32)[..., None] * inv_freq
    cos, sin = jnp.cos(ang), jnp.sin(ang)
    rest = HEAD_DIM - ROPE_DIM
    shp = ang.shape[:-1]
    cos_t = jnp.concatenate([cos, cos, jnp.ones(shp + (rest,), F32)], axis=-1)
    s1_t = jnp.concatenate([-sin, jnp.zeros(shp + (HEAD_DIM - half,), F32)], axis=-1)
    s2_t = jnp.concatenate([jnp.zeros(shp + (half,), F32), sin, jnp.zeros(shp + (rest,), F32)], axis=-1)
    return cos_t, s1_t, s2_t


def kernel(x, c, positions, w_ada, b_ada, w_in, conv_w, conv_b, attn_sink, a_log_fwd, a_log_bwd,
           dt_bias_fwd, dt_bias_bwd, ssd_d, ssd_norm_w, attn_norm_w, w_out, ln1_g, ln1_b,
           w_gate, w_up, w_down, ln2_g, ln2_b):
    assert w_ada.shape[0] == 1, "single-layer block"
    bsz, seq, d = x.shape
    m = bsz * seq
    x2 = x.reshape(m, d)

    mod = _ada(c, w_ada[0], b_ada[0])
    sh1, sc1, g1, sh2, sc2, g2 = [t.reshape(bsz, 1, d) for t in jnp.split(mod, 6, axis=-1)]

    proj, dt_raw = _inproj(_modulate(x2, sc1, sh1, seq), jnp.swapaxes(w_in[0], 0, 1))
    proj3 = proj.reshape(bsz, seq, MAIN_WIDTH)

    qr, kr = _rope(proj3, *_rope_tables(positions))
    attn = _attention(qr, kr, proj3, attn_sink[0], attn_norm_w[0])

    conv_w8 = jnp.pad(conv_w[0], ((0, 8 - CONV_WIDTH), (0, 0)))
    xbc = _conv(proj3, conv_w8, conv_b[0])
    zeros = jnp.zeros((DT_LANES - 2 * SSD_HEADS,), F32)
    par = jnp.zeros((8, LANES), F32)
    par = par.at[0].set(jnp.concatenate([dt_bias_fwd[0], dt_bias_bwd[0], zeros]))
    par = par.at[1].set(jnp.concatenate([a_log_fwd[0], a_log_bwd[0], zeros]))
    dt_raw3 = dt_raw.reshape(bsz, seq, DT_LANES)
    y_b = _ssd_bwd(xbc, dt_raw3, par)
    d_x = jnp.repeat(ssd_d[0], SSD_HEAD_DIM).reshape(SSD_GROUPS, 1, GROUP_WIDTH)
    ssd = _ssd_fwd_gate(xbc, dt_raw3, par, y_b, proj3, d_x,
                        ssd_norm_w[0].reshape(SSD_GROUPS, 1, GROUP_WIDTH))

    mix = _outproj(attn.reshape(m, ATTN_WIDTH), ssd.reshape(m, SSD_WIDTH), w_out[0])
    x1, h2 = _deepnorm(x2, mix, g1, ln1_g[0], ln1_b[0], seq, mod=(sc2, sh2))

    act, w_down_b = _ffn_up(h2, w_gate[0], w_up[0], w_down[0])
    ffn = _ffn_down(act, w_down_b)
    out = _deepnorm(x1, ffn, g2, ln2_g[0], ln2_b[0], seq)
    return out.reshape(bsz, seq, d)
```

```python
import functools

import jax
import jax.numpy as jnp
from jax import lax
from jax.experimental import pallas as pl
from jax.experimental.pallas import tpu as pltpu

F32 = jnp.float32
BF16 = jnp.bfloat16

D_MODEL = 4096
HEAD_DIM = 128
ATTN_WIDTH = D_MODEL // 2
N_Q_HEADS = ATTN_WIDTH // HEAD_DIM
N_KV_HEADS = N_Q_HEADS // 4
Q_PER_KV = N_Q_HEADS // N_KV_HEADS
KV_WIDTH = N_KV_HEADS * HEAD_DIM
WINDOW = 128
BLOCK = 128
ROPE_DIM = HEAD_DIM // 4
ROPE_THETA = 500000.0
SSD_WIDTH = D_MODEL // 2
SSD_HEAD_DIM = 64
SSD_HEADS = SSD_WIDTH // SSD_HEAD_DIM
SSD_GROUPS = 4
HEADS_PER_GROUP = SSD_HEADS // SSD_GROUPS
GROUP_WIDTH = SSD_WIDTH // SSD_GROUPS
D_STATE = 128
CONV_WIDTH = 5
CHUNK = 128
XBC_WIDTH = SSD_WIDTH + 2 * SSD_GROUPS * D_STATE
MAIN_WIDTH = ATTN_WIDTH + 2 * KV_WIDTH + SSD_WIDTH + XBC_WIDTH
D_FF = -(-8 * D_MODEL // (3 * 256)) * 256
LN_EPS = 1e-5
RMS_EPS = 1e-6
ALPHA = 2.0 ** 0.25

Q_OFF = 0
K_OFF = ATTN_WIDTH
V_OFF = K_OFF + KV_WIDTH
Z_OFF = V_OFF + KV_WIDTH
XBC_OFF = Z_OFF + SSD_WIDTH
assert MAIN_WIDTH == XBC_OFF + XBC_WIDTH

V7X_VMEM_BYTES = 64 * 1024 * 1024
VMEM_LIMIT = V7X_VMEM_BYTES * 7 // 8
LANES = 128
DT_LANES = LANES
HALO_ROWS = 16
DOT_ROWS = 1024


def _cparams(*sem):
    return pltpu.CompilerParams(dimension_semantics=sem, vmem_limit_bytes=VMEM_LIMIT)


def _silu(v):
    return v / (1.0 + jnp.exp(-v))


def _dot_f32(a, b):
    return jnp.dot(a, b, preferred_element_type=F32)


def _cond_matvec(cond_t, w):
    rows = [jnp.sum(w * cond_t[:, b:b + 1], axis=0, keepdims=True) for b in range(cond_t.shape[1])]
    return jnp.concatenate(rows, axis=0)


def _ada_head_kernel(ct_ref, w_ref, b_ref, o_ref, cond_ref):
    @pl.when(pl.program_id(0) == 0)
    def _():
        o_ref[...] = jnp.broadcast_to(b_ref[...], o_ref.shape)

    cond = _silu(ct_ref[...])
    cond_ref[...] = cond
    o_ref[...] += _cond_matvec(cond, w_ref[...])


def _ada_head(c, w_ada, b_ada2, n_head, tk=256):
    bsz, d = c.shape
    return pl.pallas_call(
        _ada_head_kernel,
        grid=(d // tk,),
        in_specs=[pl.BlockSpec((tk, bsz), lambda k: (k, 0)),
                  pl.BlockSpec((tk, n_head), lambda k: (k, 0)),
                  pl.BlockSpec((1, n_head), lambda k: (0, 0))],
        out_specs=[pl.BlockSpec((bsz, n_head), lambda k: (0, 0)),
                   pl.BlockSpec((tk, bsz), lambda k: (k, 0))],
        out_shape=[jax.ShapeDtypeStruct((bsz, n_head), F32),
                   jax.ShapeDtypeStruct((d, bsz), F32)],
        compiler_params=_cparams("arbitrary"),
        name="ada_head",
    )(c.T, w_ada, b_ada2)


def _modulate_kernel(x_ref, sc_ref, sh_ref, h_ref):
    h_ref[...] = (x_ref[...] * (1.0 + sc_ref[0]) + sh_ref[0]).astype(h_ref.dtype)


def _modulate(x2, scale, shift, seq, tm=512):
    m, d = x2.shape
    tm = min(tm, seq)
    per_batch = seq // tm
    per_b = pl.BlockSpec((1, 1, d), lambda i: (i // per_batch, 0, 0))
    return pl.pallas_call(
        _modulate_kernel,
        grid=(m // tm,),
        in_specs=[pl.BlockSpec((tm, d), lambda i: (i, 0)), per_b, per_b],
        out_specs=pl.BlockSpec((tm, d), lambda i: (i, 0)),
        out_shape=jax.ShapeDtypeStruct((m, d), BF16),
        compiler_params=_cparams("parallel"),
        name="modulate",
    )(x2, scale, shift)


def _resident(block_shape, index_map):
    return pl.BlockSpec(block_shape, index_map, pipeline_mode=pl.Buffered(1))


def _row_chunks(rows):
    step = min(rows, DOT_ROWS)
    return [slice(r, r + step) for r in range(0, rows, step)]


def _dot_nt(a, bt):
    return lax.dot_general(a, bt, (((1,), (1,)), ((), ())), preferred_element_type=F32)


def _inproj_kernel(h_ref, wt_ref, wdt_ref, cond_ref, wada_ref, bada_ref, o_ref, odt_ref, mod_ref):
    @pl.when(pl.program_id(1) == 0)
    def _():
        wdt = wdt_ref[...].astype(BF16)
        for rs in _row_chunks(h_ref.shape[0]):
            r = _dot_nt(h_ref[rs, :], wdt)
            odt_ref[rs, :] = jnp.concatenate([r, jnp.zeros((r.shape[0], DT_LANES - r.shape[1]), F32)], axis=1)

    wt = wt_ref[...].astype(BF16)
    for rs in _row_chunks(h_ref.shape[0]):
        o_ref[rs, :] = _dot_nt(h_ref[rs, :], wt).astype(o_ref.dtype)
    mod_ref[...] = _cond_matvec(cond_ref[...], wada_ref[...]) + bada_ref[...]


def _inproj(h, w_in_t, cond_t, w_ada, b_ada2, n_head, tm=2048, tn=256):
    m, d = h.shape
    tm = min(tm, m)
    nj = MAIN_WIDTH // tn
    n_dt = w_in_t.shape[0] - MAIN_WIDTH
    assert MAIN_WIDTH % n_dt == 0 and n_dt % 8 == 0
    bsz = cond_t.shape[1]
    n_tail = w_ada.shape[1] - n_head
    slab, rem = divmod(n_tail, (m // tm) * nj)
    assert rem == 0 and slab % LANES == 0 and n_head % slab == 0, (n_tail, m // tm, nj)
    slab_col = lambda i, j: (0, n_head // slab + i * nj + j)
    return pl.pallas_call(
        _inproj_kernel,
        grid=(m // tm, nj),
        in_specs=[_resident((tm, d), lambda i, j: (i, 0)),
                  pl.BlockSpec((tn, d), lambda i, j: (j, 0)),
                  pl.BlockSpec((n_dt, d), lambda i, j: (MAIN_WIDTH // n_dt, 0)),
                  pl.BlockSpec((d, bsz), lambda i, j: (0, 0)),
                  pl.BlockSpec((d, slab), slab_col),
                  pl.BlockSpec((1, slab), slab_col)],
        out_specs=[pl.BlockSpec((tm, tn), lambda i, j: (i, j)),
                   pl.BlockSpec((tm, DT_LANES), lambda i, j: (i, 0)),
                   pl.BlockSpec((bsz, slab), lambda i, j: (0, i * nj + j))],
        out_shape=[jax.ShapeDtypeStruct((m, MAIN_WIDTH), BF16),
                   jax.ShapeDtypeStruct((m, DT_LANES), F32),
                   jax.ShapeDtypeStruct((bsz, n_tail), F32)],
        compiler_params=_cparams("parallel", "arbitrary"),
        name="in_proj",
    )(h, w_in_t, w_in_t, cond_t, w_ada, b_ada2)


def _rope_kernel(q_ref, k_ref, c_ref, s1_ref, s2_ref, qo_ref, ko_ref):
    cos, s1, s2 = c_ref[0], s1_ref[0], s2_ref[0]
    half = ROPE_DIM // 2

    def rot(t):
        t = t.astype(F32)
        return (t * cos + pltpu.roll(t, HEAD_DIM - half, 1) * s1 + pltpu.roll(t, half, 1) * s2)

    for h in range(N_Q_HEADS):
        sl = slice(h * HEAD_DIM, (h + 1) * HEAD_DIM)
        qo_ref[0, :, sl] = rot(q_ref[0, :, sl]).astype(qo_ref.dtype)
    for h in range(N_KV_HEADS):
        sl = slice(h * HEAD_DIM, (h + 1) * HEAD_DIM)
        ko_ref[0, :, sl] = rot(k_ref[0, :, sl]).astype(ko_ref.dtype)


def _rope(proj3, cos_t, s1_t, s2_t, ts=256):
    bsz, seq, _ = proj3.shape
    ts = min(ts, seq)
    tab = pl.BlockSpec((1, ts, HEAD_DIM), lambda b, i: (b, i, 0))
    return pl.pallas_call(
        _rope_kernel,
        grid=(bsz, seq // ts),
        in_specs=[pl.BlockSpec((1, ts, ATTN_WIDTH), lambda b, i: (b, i, Q_OFF // ATTN_WIDTH)),
                  pl.BlockSpec((1, ts, KV_WIDTH), lambda b, i: (b, i, K_OFF // KV_WIDTH)),
                  tab, tab, tab],
        out_specs=[pl.BlockSpec((1, ts, ATTN_WIDTH), lambda b, i: (b, i, 0)),
                   pl.BlockSpec((1, ts, KV_WIDTH), lambda b, i: (b, i, 0))],
        out_shape=[jax.ShapeDtypeStruct((bsz, seq, ATTN_WIDTH), BF16),
                   jax.ShapeDtypeStruct((bsz, seq, KV_WIDTH), BF16)],
        compiler_params=_cparams("parallel", "parallel"),
        name="rope",
    )(proj3, proj3, cos_t, s1_t, s2_t)


def _attn_kernel(sink_ref, q_ref, kp_ref, kc_ref, kn_ref, vp_ref, vc_ref, vn_ref, nw_ref,
                 o_ref, acc_ref, *, seq):
    i = pl.program_id(1)
    qi = lax.broadcasted_iota(jnp.int32, (BLOCK, 3 * BLOCK), 0)
    sj = lax.broadcasted_iota(jnp.int32, (BLOCK, 3 * BLOCK), 1)
    kpos = i * BLOCK + sj - BLOCK
    keep = jnp.where(jnp.abs(sj - BLOCK - qi) <= WINDOW,
                     jnp.where(kpos >= 0, jnp.where(kpos < seq, 1, 0), 0), 0)
    bias = jnp.where(keep == 1, 0.0, -jnp.inf).astype(F32)
    scale = HEAD_DIM ** -0.5
    ssq = jnp.zeros((BLOCK, 1), F32)
    for kv in range(N_KV_HEADS):
        ksl = slice(kv * HEAD_DIM, (kv + 1) * HEAD_DIM)
        kwin = jnp.concatenate([kp_ref[0, :, ksl], kc_ref[0, :, ksl], kn_ref[0, :, ksl]], axis=0)
        vwin = jnp.concatenate([vp_ref[0, :, ksl], vc_ref[0, :, ksl], vn_ref[0, :, ksl]], axis=0)
        q4 = jnp.concatenate(
            [q_ref[0, :, (kv * Q_PER_KV + g) * HEAD_DIM:(kv * Q_PER_KV + g + 1) * HEAD_DIM]
             for g in range(Q_PER_KV)], axis=0)
        s = lax.dot_general(q4, kwin, (((1,), (1,)), ((), ())), preferred_element_type=F32)
        probs = []
        for g in range(Q_PER_KV):
            sink = sink_ref[kv * Q_PER_KV + g]
            sg = s[g * BLOCK:(g + 1) * BLOCK] * scale + bias
            m = jnp.maximum(jnp.max(sg, axis=-1, keepdims=True), sink)
            p = jnp.exp(sg - m)
            denom = jnp.sum(p, axis=-1, keepdims=True) + jnp.exp(sink - m)
            probs.append((p / denom).astype(BF16))
        o = jnp.dot(jnp.concatenate(probs, axis=0), vwin, preferred_element_type=F32)
        for g in range(Q_PER_KV):
            og = o[g * BLOCK:(g + 1) * BLOCK]
            h = kv * Q_PER_KV + g
            acc_ref[:, h * HEAD_DIM:(h + 1) * HEAD_DIM] = og
            ssq = ssq + jnp.sum(og * og, axis=-1, keepdims=True)
    inv = lax.rsqrt(ssq / ATTN_WIDTH + RMS_EPS)
    o_ref[0] = (acc_ref[...] * inv * nw_ref[...]).astype(o_ref.dtype)


def _attention(qr, kr, proj3, sink, norm_w):
    bsz, seq, _ = qr.shape
    nb = seq // BLOCK
    vcol = V_OFF // KV_WIDTH

    def kv_specs(col):
        return [pl.BlockSpec((1, BLOCK, KV_WIDTH), lambda b, i: (b, jnp.maximum(i - 1, 0), col)),
                pl.BlockSpec((1, BLOCK, KV_WIDTH), lambda b, i: (b, i, col)),
                pl.BlockSpec((1, BLOCK, KV_WIDTH), lambda b, i: (b, jnp.minimum(i + 1, nb - 1), col))]

    return pl.pallas_call(
        functools.partial(_attn_kernel, seq=seq),
        grid=(bsz, nb),
        in_specs=[pl.BlockSpec(memory_space=pltpu.SMEM),
                  pl.BlockSpec((1, BLOCK, ATTN_WIDTH), lambda b, i: (b, i, 0))]
                 + kv_specs(0) + kv_specs(vcol)
                 + [pl.BlockSpec((1, ATTN_WIDTH), lambda b, i: (0, 0))],
        out_specs=pl.BlockSpec((1, BLOCK, ATTN_WIDTH), lambda b, i: (b, i, 0)),
        out_shape=jax.ShapeDtypeStruct((bsz, seq, ATTN_WIDTH), BF16),
        scratch_shapes=[pltpu.VMEM((BLOCK, ATTN_WIDTH), F32)],
        compiler_params=_cparams("parallel", "parallel"),
        name="band_attn",
    )(sink, qr, kr, kr, kr, proj3, proj3, proj3, norm_w.reshape(1, ATTN_WIDTH))


def _conv_kernel(prev_ref, cur_ref, next_ref, w_ref, b_ref, o_ref):
    i = pl.program_id(1)
    last = pl.num_programs(1) - 1
    ts = cur_ref.shape[1]
    pad = CONV_WIDTH // 2
    prev = jnp.where(i > 0, prev_ref[0].astype(F32), 0.0)
    nxt = jnp.where(i < last, next_ref[0].astype(F32), 0.0)
    ext = jnp.concatenate([prev, cur_ref[0].astype(F32), nxt], axis=0)
    rows = ext.shape[0]
    w = w_ref[...]
    acc = jnp.broadcast_to(b_ref[...], (ts, ext.shape[1]))
    for j in range(CONV_WIDTH):
        d = j - pad
        shifted = ext if d == 0 else pltpu.roll(ext, (-d) % rows, 0)
        acc = acc + shifted[HALO_ROWS:HALO_ROWS + ts] * w[j:j + 1, :]
    o_ref[0] = _silu(acc).astype(o_ref.dtype)


def _conv(proj3, conv_w8, conv_b, ts=512, tc=1024):
    bsz, seq, _ = proj3.shape
    ts = min(ts, seq)
    col0 = XBC_OFF // tc
    hb = ts // HALO_ROWS
    nhalo = seq // HALO_ROWS
    return pl.pallas_call(
        _conv_kernel,
        grid=(bsz, seq // ts, XBC_WIDTH // tc),
        in_specs=[pl.BlockSpec((1, HALO_ROWS, tc), lambda b, i, j: (b, jnp.maximum(i * hb - 1, 0), col0 + j)),
                  pl.BlockSpec((1, ts, tc), lambda b, i, j: (b, i, col0 + j)),
                  pl.BlockSpec((1, HALO_ROWS, tc),
                               lambda b, i, j: (b, jnp.minimum((i + 1) * hb, nhalo - 1), col0 + j)),
                  pl.BlockSpec((8, tc), lambda b, i, j: (0, j)),
                  pl.BlockSpec((1, tc), lambda b, i, j: (0, j))],
        out_specs=pl.BlockSpec((1, ts, tc), lambda b, i, j: (b, i, j)),
        out_shape=jax.ShapeDtypeStruct((bsz, seq, XBC_WIDTH), BF16),
        compiler_params=_cparams("parallel", "parallel", "parallel"),
        name="dwconv_silu",
    )(proj3, proj3, proj3, conv_w8, conv_b.reshape(1, XBC_WIDTH))


def _bf16_pieces(v, parts):
    pieces = []
    rem = v
    for p in range(parts):
        piece = rem.astype(BF16)
        pieces.append(piece)
        if p + 1 < parts:
            rem = rem - piece.astype(F32)
    return pieces


def _ssd_chunk(xs_ref, b_ref, c_ref, dt_ref, par_ref, state_ref, emit, reverse):
    L = CHUNK

    @pl.when(pl.program_id(1) == 0)
    def _():
        state_ref[...] = jnp.zeros_like(state_ref)

    row = lax.broadcasted_iota(jnp.int32, (L, L), 0)
    col = lax.broadcasted_iota(jnp.int32, (L, L), 1)
    causal = (col >= row) if reverse else (col <= row)
    tri_b = jnp.where(causal, 1.0, 0.0).astype(BF16)

    xb = dt_ref[0] + par_ref[0:1, :]
    dt = jnp.maximum(xb, 0.0) + jnp.log1p(jnp.exp(-jnp.abs(xb)))
    adt = dt * (-jnp.exp(par_ref[1:2, :]))
    acum = sum(_dot_f32(tri_b, piece) for piece in _bf16_pieces(adt, 3))
    acum_t = acum.T
    edge = acum[0:1, :] if reverse else acum[L - 1:L, :]
    stack = jnp.concatenate([dt, jnp.exp(acum), jnp.exp(edge - acum)], axis=0)

    lane = lax.broadcasted_iota(jnp.int32, (L, LANES), 1)
    head_off = SSD_HEADS if reverse else 0
    for g in range(SSD_GROUPS):
        base = head_off + g * HEADS_PER_GROUP
        gsl = slice(g * GROUP_WIDTH, (g + 1) * GROUP_WIDTH)
        er = lax.broadcasted_iota(jnp.int32, (LANES, GROUP_WIDTH), 0)
        ec = lax.broadcasted_iota(jnp.int32, (LANES, GROUP_WIDTH), 1) // SSD_HEAD_DIM
        expand_b = jnp.where(er == ec + base, 1.0, 0.0).astype(BF16)
        ex = sum(_dot_f32(piece, expand_b) for piece in _bf16_pieces(stack, 2))
        dt_x, eac_x, ds_x = ex[0:L], ex[L:2 * L], ex[2 * L:3 * L]
        xs = xs_ref[0, :, gsl].astype(F32)
        xdt = xs * dt_x
        bm = b_ref[0, :, g * D_STATE:(g + 1) * D_STATE]
        cm = c_ref[0, :, g * D_STATE:(g + 1) * D_STATE]
        cb = lax.dot_general(cm, bm, (((1,), (1,)), ((), ())), preferred_element_type=F32)
        ys = []
        for pair in range(HEADS_PER_GROUP // 2):
            ms = []
            for r in (2 * pair, 2 * pair + 1):
                hl = base + r
                seg = acum[:, hl:hl + 1] - acum_t[hl:hl + 1, :]
                ms.append((cb * jnp.exp(jnp.where(causal, seg, -jnp.inf))).astype(BF16))
            xp = xdt[:, pair * LANES:(pair + 1) * LANES]
            rhs = jnp.concatenate([jnp.where(lane < SSD_HEAD_DIM, xp, 0.0),
                                   jnp.where(lane >= SSD_HEAD_DIM, xp, 0.0)], axis=0).astype(BF16)
            ys.append(jnp.dot(jnp.concatenate(ms, axis=1), rhs, preferred_element_type=F32))
        y_diag = jnp.concatenate(ys, axis=1)
        st = state_ref[g]
        y_off = jnp.dot(cm, st.astype(BF16), preferred_element_type=F32) * eac_x
        emit(g, gsl, xs, y_diag + y_off)
        new = lax.dot_general(bm, (xdt * ds_x).astype(BF16), (((0,), (0,)), ((), ())),
                              preferred_element_type=F32)
        cdec = eac_x[0:1, :] if reverse else eac_x[L - 1:L, :]
        state_ref[g] = st * cdec + new


def _ssd_bwd_kernel(xs_ref, b_ref, c_ref, dt_ref, par_ref, y_ref, state_ref):
    def emit(g, gsl, xs, y):
        y_ref[0, :, gsl] = y

    _ssd_chunk(xs_ref, b_ref, c_ref, dt_ref, par_ref, state_ref, emit, reverse=True)


def _ssd_fwd_gate_kernel(xs_ref, b_ref, c_ref, dt_ref, par_ref, yb_ref, zlo_ref, zhi_ref, d_ref, nw_ref,
                         o_ref, state_ref):
    half = SSD_GROUPS // 2

    def emit(g, gsl, xs, y):
        z_ref = zlo_ref if g < half else zhi_ref
        zsl = slice((g % half) * GROUP_WIDTH, (g % half + 1) * GROUP_WIDTH)
        y = y + yb_ref[0, :, gsl] + d_ref[g] * xs
        y = y * _silu(z_ref[0, :, zsl].astype(F32))
        y = y * lax.rsqrt(jnp.mean(y * y, axis=-1, keepdims=True) + RMS_EPS)
        o_ref[0, :, gsl] = (y * nw_ref[g]).astype(o_ref.dtype)

    _ssd_chunk(xs_ref, b_ref, c_ref, dt_ref, par_ref, state_ref, emit, reverse=False)


def _ssd_specs(nc, reverse):
    bc_w = SSD_GROUPS * D_STATE
    cidx = (lambda c: nc - 1 - c) if reverse else (lambda c: c)
    chunk = lambda width, col: pl.BlockSpec((1, CHUNK, width), lambda b, c: (b, cidx(c), col))
    specs = [chunk(SSD_WIDTH, 0), chunk(bc_w, SSD_WIDTH // bc_w), chunk(bc_w, SSD_WIDTH // bc_w + 1),
             chunk(DT_LANES, 0), pl.BlockSpec((8, LANES), lambda b, c: (0, 0))]
    return specs, chunk


def _ssd_bwd(xbc, dt_raw3, par):
    bsz, seq, _ = xbc.shape
    specs, chunk = _ssd_specs(seq // CHUNK, reverse=True)
    return pl.pallas_call(
        _ssd_bwd_kernel,
        grid=(bsz, seq // CHUNK),
        in_specs=specs,
        out_specs=chunk(SSD_WIDTH, 0),
        out_shape=jax.ShapeDtypeStruct((bsz, seq, SSD_WIDTH), F32),
        scratch_shapes=[pltpu.VMEM((SSD_GROUPS, D_STATE, GROUP_WIDTH), F32)],
        compiler_params=_cparams("parallel", "arbitrary"),
        name="ssd_bwd",
    )(xbc, xbc, xbc, dt_raw3, par)


def _ssd_fwd_gate(xbc, dt_raw3, par, y_b, proj3, d_x, norm_w):
    bsz, seq, _ = xbc.shape
    specs, chunk = _ssd_specs(seq // CHUNK, reverse=False)
    zw = SSD_WIDTH // 2
    per_group = pl.BlockSpec((SSD_GROUPS, 1, GROUP_WIDTH), lambda b, c: (0, 0, 0))
    return pl.pallas_call(
        _ssd_fwd_gate_kernel,
        grid=(bsz, seq // CHUNK),
        in_specs=specs + [chunk(SSD_WIDTH, 0), chunk(zw, Z_OFF // zw), chunk(zw, Z_OFF // zw + 1),
                          per_group, per_group],
        out_specs=chunk(SSD_WIDTH, 0),
        out_shape=jax.ShapeDtypeStruct((bsz, seq, SSD_WIDTH), BF16),
        scratch_shapes=[pltpu.VMEM((SSD_GROUPS, D_STATE, GROUP_WIDTH), F32)],
        compiler_params=_cparams("parallel", "arbitrary"),
        name="ssd_fwd_gate",
    )(xbc, xbc, xbc, dt_raw3, par, y_b, proj3, proj3, d_x, norm_w)


def _outproj_kernel(a_ref, s_ref, wa_ref, ws_ref, o_ref):
    wa = wa_ref[...].astype(BF16)
    ws = ws_ref[...].astype(BF16)
    for rs in _row_chunks(a_ref.shape[0]):
        o_ref[rs, :] = _dot_f32(a_ref[rs, :], wa) + _dot_f32(s_ref[rs, :], ws)


def _outproj(attn2, ssd2, w_out, tm=2048, tn=256):
    m, ka = attn2.shape
    n = w_out.shape[1]
    tm = min(tm, m)
    return pl.pallas_call(
        _outproj_kernel,
        grid=(m // tm, n // tn),
        in_specs=[pl.BlockSpec((tm, ka), lambda i, j: (i, 0)),
                  pl.BlockSpec((tm, ka), lambda i, j: (i, 0)),
                  pl.BlockSpec((ka, tn), lambda i, j: (0, j)),
                  pl.BlockSpec((ka, tn), lambda i, j: (1, j))],
        out_specs=pl.BlockSpec((tm, tn), lambda i, j: (i, j)),
        out_shape=jax.ShapeDtypeStruct((m, n), F32),
        compiler_params=_cparams("parallel", "arbitrary"),
        name="out_proj",
    )(attn2, ssd2, w_out, w_out)


def _ln_body(x_ref, br_ref, gate_ref, g_ref, b_ref):
    v = ALPHA * x_ref[...] + gate_ref[0] * br_ref[...]
    mu = jnp.mean(v, axis=-1, keepdims=True)
    vc = v - mu
    var = jnp.mean(vc * vc, axis=-1, keepdims=True)
    return vc * lax.rsqrt(var + LN_EPS) * g_ref[...] + b_ref[...]


def _ln_mod_kernel(x_ref, br_ref, gate_ref, g_ref, b_ref, sc_ref, sh_ref, o_ref, h_ref):
    out = _ln_body(x_ref, br_ref, gate_ref, g_ref, b_ref)
    o_ref[...] = out
    h_ref[...] = (out * (1.0 + sc_ref[0]) + sh_ref[0]).astype(h_ref.dtype)


def _ln_kernel(x_ref, br_ref, gate_ref, g_ref, b_ref, o_ref):
    o_ref[...] = _ln_body(x_ref, br_ref, gate_ref, g_ref, b_ref)


def _deepnorm(x2, branch, gate, ln_g, ln_b, seq, mod=None, tm=256):
    m, d = x2.shape
    tm = min(tm, seq)
    per_batch = seq // tm
    rows = pl.BlockSpec((tm, d), lambda i: (i, 0))
    per_b = pl.BlockSpec((1, 1, d), lambda i: (i // per_batch, 0, 0))
    vec = pl.BlockSpec((1, d), lambda i: (0, 0))
    args = [x2, branch, gate, ln_g.reshape(1, d), ln_b.reshape(1, d)]
    in_specs = [rows, rows, per_b, vec, vec]
    if mod is None:
        return pl.pallas_call(
            _ln_kernel, grid=(m // tm,), in_specs=in_specs, out_specs=rows,
            out_shape=jax.ShapeDtypeStruct((m, d), F32),
            compiler_params=_cparams("parallel"), name="deepnorm_ln",
        )(*args)
    return pl.pallas_call(
        _ln_mod_kernel, grid=(m // tm,), in_specs=in_specs + [per_b, per_b],
        out_specs=[rows, rows],
        out_shape=[jax.ShapeDtypeStruct((m, d), F32), jax.ShapeDtypeStruct((m, d), BF16)],
        compiler_params=_cparams("parallel"), name="deepnorm_ln_mod",
    )(*args, *mod)


def _ffn_up_kernel(h_ref, wg_ref, wu_ref, wd_ref, o_ref, wdo_ref):
    tn = wg_ref.shape[1]
    w = jnp.concatenate([wg_ref[...].astype(BF16), wu_ref[...].astype(BF16)], axis=1)
    for rs in _row_chunks(h_ref.shape[0]):
        r = _dot_f32(h_ref[rs, :], w)
        o_ref[rs, :] = (_silu(r[:, :tn]) * r[:, tn:]).astype(o_ref.dtype)
    wdo_ref[...] = wd_ref[...].astype(wdo_ref.dtype)


def _ffn_up(h2, w_gate, w_up, w_down, tm=2048, tn=256):
    m, d = h2.shape
    n = w_gate.shape[1]
    tm = min(tm, m)
    nj = n // tn
    steps = (m // tm) * nj
    slab, rem = divmod(w_down.shape[0], steps)
    assert rem == 0 and slab % HALO_ROWS == 0, (w_down.shape, steps)
    wspec = pl.BlockSpec((d, tn), lambda i, j: (0, j))
    slab_spec = pl.BlockSpec((slab, w_down.shape[1]), lambda i, j: (i * nj + j, 0))
    return pl.pallas_call(
        _ffn_up_kernel,
        grid=(m // tm, nj),
        in_specs=[_resident((tm, d), lambda i, j: (i, 0)), wspec, wspec, slab_spec],
        out_specs=[pl.BlockSpec((tm, tn), lambda i, j: (i, j)), slab_spec],
        out_shape=[jax.ShapeDtypeStruct((m, n), BF16),
                   jax.ShapeDtypeStruct(w_down.shape, BF16)],
        compiler_params=_cparams("parallel", "arbitrary"),
        name="ffn_up",
    )(h2, w_gate, w_up, w_down)


def _mm_kernel(a_ref, w_ref, o_ref):
    o_ref[...] = jnp.dot(a_ref[...], w_ref[...], preferred_element_type=F32).astype(o_ref.dtype)


def _ffn_down(act, w_down, tm=512, tn=512):
    m, k = act.shape
    n = w_down.shape[1]
    tm = min(tm, m)
    return pl.pallas_call(
        _mm_kernel,
        grid=(m // tm, n // tn),
        in_specs=[pl.BlockSpec((tm, k), lambda i, j: (i, 0)),
                  pl.BlockSpec((k, tn), lambda i, j: (0, j))],
        out_specs=pl.BlockSpec((tm, tn), lambda i, j: (i, j)),
        out_shape=jax.ShapeDtypeStruct((m, n), F32),
        compiler_params=_cparams("parallel", "arbitrary"),
        name="ffn_down",
    )(act, w_down)


def _rope_tables(positions):
    half = ROPE_DIM // 2
    inv_freq = ROPE_THETA ** (-jnp.arange(0, ROPE_DIM, 2, dtype=F32) / ROPE_DIM)
    ang = positions.astype(F32)[..., None] * inv_freq
    cos, sin = jnp.cos(ang), jnp.sin(ang)
    rest = HEAD_DIM - ROPE_DIM
    shp = ang.shape[:-1]
    cos_t = jnp.concatenate([cos, cos, jnp.ones(shp + (rest,), F32)], axis=-1)
    s1_t = jnp.concatenate([-sin, jnp.zeros(shp + (HEAD_DIM - half,), F32)], axis=-1)
    s2_t = jnp.concatenate([jnp.zeros(shp + (half,), F32), sin, jnp.zeros(shp + (rest,), F32)], axis=-1)
    return cos_t, s1_t, s2_t


def kernel(x, c, positions, w_ada, b_ada, w_in, conv_w, conv_b, attn_sink, a_log_fwd, a_log_bwd,
           dt_bias_fwd, dt_bias_bwd, ssd_d, ssd_norm_w, attn_norm_w, w_out, ln1_g, ln1_b,
           w_gate, w_up, w_down, ln2_g, ln2_b):
    assert w_ada.shape[0] == 1, "single-layer block"
    bsz, seq, d = x.shape
    m = bsz * seq
    x2 = x.reshape(m, d)

    b_ada2 = b_ada[0].reshape(1, 6 * d)
    mod_head, cond_t = _ada_head(c, w_ada[0], b_ada2, 2 * d)
    sh1, sc1 = [t.reshape(bsz, 1, d) for t in jnp.split(mod_head, 2, axis=-1)]

    proj, dt_raw, mod_tail = _inproj(_modulate(x2, sc1, sh1, seq), jnp.swapaxes(w_in[0], 0, 1),
                                     cond_t, w_ada[0], b_ada2, 2 * d)
    g1, sh2, sc2, g2 = [t.reshape(bsz, 1, d) for t in jnp.split(mod_tail, 4, axis=-1)]
    proj3 = proj.reshape(bsz, seq, MAIN_WIDTH)

    qr, kr = _rope(proj3, *_rope_tables(positions))
    attn = _attention(qr, kr, proj3, attn_sink[0], attn_norm_w[0])

    conv_w8 = jnp.pad(conv_w[0], ((0, 8 - CONV_WIDTH), (0, 0)))
    xbc = _conv(proj3, conv_w8, conv_b[0])
    zeros = jnp.zeros((DT_LANES - 2 * SSD_HEADS,), F32)
    par = jnp.zeros((8, LANES), F32)
    par = par.at[0].set(jnp.concatenate([dt_bias_fwd[0], dt_bias_bwd[0], zeros]))
    par = par.at[1].set(jnp.concatenate([a_log_fwd[0], a_log_bwd[0], zeros]))
    dt_raw3 = dt_raw.reshape(bsz, seq, DT_LANES)
    y_b = _ssd_bwd(xbc, dt_raw3, par)
    d_x = jnp.repeat(ssd_d[0], SSD_HEAD_DIM).reshape(SSD_GROUPS, 1, GROUP_WIDTH)
    ssd = _ssd_fwd_gate(xbc, dt_raw3, par, y_b, proj3, d_x,
                        ssd_norm_w[0].reshape(SSD_GROUPS, 1, GROUP_WIDTH))

    mix = _outproj(attn.reshape(m, ATTN_WIDTH), ssd.reshape(m, SSD_WIDTH), w_out[0])
    x1, h2 = _deepnorm(x2, mix, g1, ln1_g[0], ln1_b[0], seq, mod=(sc2, sh2))

    act, w_down_b = _ffn_up(h2, w_gate[0], w_up[0], w_down[0])
    ffn = _ffn_down(act, w_down_b)
    out = _deepnorm(x1, ffn, g2, ln2_g[0], ln2_b[0], seq)
    return out.reshape(bsz, seq, d)
```

```python
import functools

import jax
import jax.numpy as jnp
from jax import lax
from jax.experimental import pallas as pl
from jax.experimental.pallas import tpu as pltpu

F32 = jnp.float32
BF16 = jnp.bfloat16

D_MODEL = 4096
HEAD_DIM = 128
ATTN_WIDTH = D_MODEL // 2
N_Q_HEADS = ATTN_WIDTH // HEAD_DIM
N_KV_HEADS = N_Q_HEADS // 4
Q_PER_KV = N_Q_HEADS // N_KV_HEADS
KV_WIDTH = N_KV_HEADS * HEAD_DIM
WINDOW = 128
BLOCK = 128
ROPE_DIM = HEAD_DIM // 4
ROPE_THETA = 500000.0
SSD_WIDTH = D_MODEL // 2
SSD_HEAD_DIM = 64
SSD_HEADS = SSD_WIDTH // SSD_HEAD_DIM
SSD_GROUPS = 4
HEADS_PER_GROUP = SSD_HEADS // SSD_GROUPS
GROUP_WIDTH = SSD_WIDTH // SSD_GROUPS
D_STATE = 128
CONV_WIDTH = 5
CHUNK = 128
XBC_WIDTH = SSD_WIDTH + 2 * SSD_GROUPS * D_STATE
MAIN_WIDTH = ATTN_WIDTH + 2 * KV_WIDTH + SSD_WIDTH + XBC_WIDTH
D_FF = -(-8 * D_MODEL // (3 * 256)) * 256
LN_EPS = 1e-5
RMS_EPS = 1e-6
ALPHA = 2.0 ** 0.25

Q_OFF = 0
K_OFF = ATTN_WIDTH
V_OFF = K_OFF + KV_WIDTH
Z_OFF = V_OFF + KV_WIDTH
XBC_OFF = Z_OFF + SSD_WIDTH
assert MAIN_WIDTH == XBC_OFF + XBC_WIDTH

V7X_VMEM_BYTES = 64 * 1024 * 1024
VMEM_LIMIT = V7X_VMEM_BYTES * 7 // 8
LANES = 128
DT_LANES = LANES
HALO_ROWS = 16
DOT_ROWS = 1024


def _cparams(*sem):
    return pltpu.CompilerParams(dimension_semantics=sem, vmem_limit_bytes=VMEM_LIMIT)


def _silu(v):
    return v / (1.0 + jnp.exp(-v))


def _dot_f32(a, b):
    return jnp.dot(a, b, preferred_element_type=F32)


def _cond_matvec(cond_t, w):
    rows = [jnp.sum(w * cond_t[:, b:b + 1], axis=0, keepdims=True) for b in range(cond_t.shape[1])]
    return jnp.concatenate(rows, axis=0)


def _ada_head_kernel(ct_ref, w_ref, b_ref, o_ref, cond_ref):
    @pl.when(pl.program_id(0) == 0)
    def _():
        o_ref[...] = jnp.broadcast_to(b_ref[...], o_ref.shape)

    cond = _silu(ct_ref[...])
    cond_ref[...] = cond
    o_ref[...] += _cond_matvec(cond, w_ref[...])


def _ada_head(c, w_ada, b_ada2, n_head, tk=256):
    bsz, d = c.shape
    return pl.pallas_call(
        _ada_head_kernel,
        grid=(d // tk,),
        in_specs=[pl.BlockSpec((tk, bsz), lambda k: (k, 0)),
                  pl.BlockSpec((tk, n_head), lambda k: (k, 0)),
                  pl.BlockSpec((1, n_head), lambda k: (0, 0))],
        out_specs=[pl.BlockSpec((bsz, n_head), lambda k: (0, 0)),
                   pl.BlockSpec((tk, bsz), lambda k: (k, 0))],
        out_shape=[jax.ShapeDtypeStruct((bsz, n_head), F32),
                   jax.ShapeDtypeStruct((d, bsz), F32)],
        compiler_params=_cparams("arbitrary"),
        name="ada_head",
    )(c.T, w_ada, b_ada2)


def _modulate_kernel(x_ref, sc_ref, sh_ref, h_ref):
    h_ref[...] = (x_ref[...] * (1.0 + sc_ref[0]) + sh_ref[0]).astype(h_ref.dtype)


def _modulate(x2, scale, shift, seq, tm=512):
    m, d = x2.shape
    tm = min(tm, seq)
    per_batch = seq // tm
    per_b = pl.BlockSpec((1, 1, d), lambda i: (i // per_batch, 0, 0))
    return pl.pallas_call(
        _modulate_kernel,
        grid=(m // tm,),
        in_specs=[pl.BlockSpec((tm, d), lambda i: (i, 0)), per_b, per_b],
        out_specs=pl.BlockSpec((tm, d), lambda i: (i, 0)),
        out_shape=jax.ShapeDtypeStruct((m, d), BF16),
        compiler_params=_cparams("parallel"),
        name="modulate",
    )(x2, scale, shift)


def _resident(block_shape, index_map):
    return pl.BlockSpec(block_shape, index_map, pipeline_mode=pl.Buffered(1))


def _row_chunks(rows):
    step = min(rows, DOT_ROWS)
    return [slice(r, r + step) for r in range(0, rows, step)]


def _dot_nt(a, bt):
    return lax.dot_general(a, bt, (((1,), (1,)), ((), ())), preferred_element_type=F32)


def _inproj_kernel(h_ref, wt_ref, wdt_ref, cond_ref, wada_ref, bada_ref, o_ref, odt_ref, mod_ref):
    @pl.when(pl.program_id(1) == 0)
    def _():
        wdt = wdt_ref[...].astype(BF16)
        for rs in _row_chunks(h_ref.shape[0]):
            r = _dot_nt(h_ref[rs, :], wdt)
            odt_ref[rs, :] = jnp.concatenate([r, jnp.zeros((r.shape[0], DT_LANES - r.shape[1]), F32)], axis=1)

    wt = wt_ref[...].astype(BF16)
    for rs in _row_chunks(h_ref.shape[0]):
        o_ref[rs, :] = _dot_nt(h_ref[rs, :], wt).astype(o_ref.dtype)
    mod_ref[...] = _cond_matvec(cond_ref[...], wada_ref[...]) + bada_ref[...]


def _inproj(h, w_in_t, cond_t, w_ada, b_ada2, n_head, tm=2048, tn=256):
    m, d = h.shape
    tm = min(tm, m)
    nj = MAIN_WIDTH // tn
    n_dt = w_in_t.shape[0] - MAIN_WIDTH
    assert MAIN_WIDTH % n_dt == 0 and n_dt % 8 == 0
    bsz = cond_t.shape[1]
    n_tail = w_ada.shape[1] - n_head
    slab, rem = divmod(n_tail, (m // tm) * nj)
    assert rem == 0 and slab % LANES == 0 and n_head % slab == 0, (n_tail, m // tm, nj)
    slab_col = lambda i, j: (0, n_head // slab + i * nj + j)
    return pl.pallas_call(
        _inproj_kernel,
        grid=(m // tm, nj),
        in_specs=[_resident((tm, d), lambda i, j: (i, 0)),
                  pl.BlockSpec((tn, d), lambda i, j: (j, 0)),
                  pl.BlockSpec((n_dt, d), lambda i, j: (MAIN_WIDTH // n_dt, 0)),
                  pl.BlockSpec((d, bsz), lambda i, j: (0, 0)),
                  pl.BlockSpec((d, slab), slab_col),
                  pl.BlockSpec((1, slab), slab_col)],
        out_specs=[pl.BlockSpec((tm, tn), lambda i, j: (i, j)),
                   pl.BlockSpec((tm, DT_LANES), lambda i, j: (i, 0)),
                   pl.BlockSpec((bsz, slab), lambda i, j: (0, i * nj + j))],
        out_shape=[jax.ShapeDtypeStruct((m, MAIN_WIDTH), BF16),
                   jax.ShapeDtypeStruct((m, DT_LANES), F32),
                   jax.ShapeDtypeStruct((bsz, n_tail), F32)],
        compiler_params=_cparams("parallel", "arbitrary"),
        name="in_proj",
    )(h, w_in_t, w_in_t, cond_t, w_ada, b_ada2)


def _rope_kernel(q_ref, k_ref, c_ref, s1_ref, s2_ref, qo_ref, ko_ref):
    cos, s1, s2 = c_ref[0], s1_ref[0], s2_ref[0]
    half = ROPE_DIM // 2

    def rot(t):
        t = t.astype(F32)
        return (t * cos + pltpu.roll(t, HEAD_DIM - half, 1) * s1 + pltpu.roll(t, half, 1) * s2)

    for h in range(N_Q_HEADS):
        sl = slice(h * HEAD_DIM, (h + 1) * HEAD_DIM)
        qo_ref[0, :, sl] = rot(q_ref[0, :, sl]).astype(qo_ref.dtype)
    for h in range(N_KV_HEADS):
        sl = slice(h * HEAD_DIM, (h + 1) * HEAD_DIM)
        ko_ref[0, :, sl] = rot(k_ref[0, :, sl]).astype(ko_ref.dtype)


def _rope(proj3, cos_t, s1_t, s2_t, ts=256):
    bsz, seq, _ = proj3.shape
    ts = min(ts, seq)
    tab = pl.BlockSpec((1, ts, HEAD_DIM), lambda b, i: (b, i, 0))
    return pl.pallas_call(
        _rope_kernel,
        grid=(bsz, seq // ts),
        in_specs=[pl.BlockSpec((1, ts, ATTN_WIDTH), lambda b, i: (b, i, Q_OFF // ATTN_WIDTH)),
                  pl.BlockSpec((1, ts, KV_WIDTH), lambda b, i: (b, i, K_OFF // KV_WIDTH)),
                  tab, tab, tab],
        out_specs=[pl.BlockSpec((1, ts, ATTN_WIDTH), lambda b, i: (b, i, 0)),
                   pl.BlockSpec((1, ts, KV_WIDTH), lambda b, i: (b, i, 0))],
        out_shape=[jax.ShapeDtypeStruct((bsz, seq, ATTN_WIDTH), BF16),
                   jax.ShapeDtypeStruct((bsz, seq, KV_WIDTH), BF16)],
        compiler_params=_cparams("parallel", "parallel"),
        name="rope",
    )(proj3, proj3, cos_t, s1_t, s2_t)


def _attn_kernel(sink_ref, q_ref, kp_ref, kc_ref, kn_ref, vp_ref, vc_ref, vn_ref, nw_ref,
                 o_ref, acc_ref, *, seq):
    i = pl.program_id(1)
    qi = lax.broadcasted_iota(jnp.int32, (BLOCK, 3 * BLOCK), 0)
    sj = lax.broadcasted_iota(jnp.int32, (BLOCK, 3 * BLOCK), 1)
    kpos = i * BLOCK + sj - BLOCK
    keep = jnp.where(jnp.abs(sj - BLOCK - qi) <= WINDOW,
                     jnp.where(kpos >= 0, jnp.where(kpos < seq, 1, 0), 0), 0)
    bias = jnp.where(keep == 1, 0.0, -jnp.inf).astype(F32)
    scale = HEAD_DIM ** -0.5
    ssq = jnp.zeros((BLOCK, 1), F32)
    for kv in range(N_KV_HEADS):
        ksl = slice(kv * HEAD_DIM, (kv + 1) * HEAD_DIM)
        kwin = jnp.concatenate([kp_ref[0, :, ksl], kc_ref[0, :, ksl], kn_ref[0, :, ksl]], axis=0)
        vwin = jnp.concatenate([vp_ref[0, :, ksl], vc_ref[0, :, ksl], vn_ref[0, :, ksl]], axis=0)
        q4 = jnp.concatenate(
            [q_ref[0, :, (kv * Q_PER_KV + g) * HEAD_DIM:(kv * Q_PER_KV + g + 1) * HEAD_DIM]
             for g in range(Q_PER_KV)], axis=0)
        s = lax.dot_general(q4, kwin, (((1,), (1,)), ((), ())), preferred_element_type=F32)
        probs = []
        for g in range(Q_PER_KV):
            sink = sink_ref[kv * Q_PER_KV + g]
            sg = s[g * BLOCK:(g + 1) * BLOCK] * scale + bias
            m = jnp.maximum(jnp.max(sg, axis=-1, keepdims=True), sink)
            p = jnp.exp(sg - m)
            denom = jnp.sum(p, axis=-1, keepdims=True) + jnp.exp(sink - m)
            probs.append((p / denom).astype(BF16))
        o = jnp.dot(jnp.concatenate(probs, axis=0), vwin, preferred_element_type=F32)
        for g in range(Q_PER_KV):
            og = o[g * BLOCK:(g + 1) * BLOCK]
            h = kv * Q_PER_KV + g
            acc_ref[:, h * HEAD_DIM:(h + 1) * HEAD_DIM] = og
            ssq = ssq + jnp.sum(og * og, axis=-1, keepdims=True)
    inv = lax.rsqrt(ssq / ATTN_WIDTH + RMS_EPS)
    o_ref[0] = (acc_ref[...] * inv * nw_ref[...]).astype(o_ref.dtype)


def _attention(qr, kr, proj3, sink, norm_w):
    bsz, seq, _ = qr.shape
    nb = seq // BLOCK
    vcol = V_OFF // KV_WIDTH

    def kv_specs(col):
        return [pl.BlockSpec((1, BLOCK, KV_WIDTH), lambda b, i: (b, jnp.maximum(i - 1, 0), col)),
                pl.BlockSpec((1, BLOCK, KV_WIDTH), lambda b, i: (b, i, col)),
                pl.BlockSpec((1, BLOCK, KV_WIDTH), lambda b, i: (b, jnp.minimum(i + 1, nb - 1), col))]

    return pl.pallas_call(
        functools.partial(_attn_kernel, seq=seq),
        grid=(bsz, nb),
        in_specs=[pl.BlockSpec(memory_space=pltpu.SMEM),
                  pl.BlockSpec((1, BLOCK, ATTN_WIDTH), lambda b, i: (b, i, 0))]
                 + kv_specs(0) + kv_specs(vcol)
                 + [pl.BlockSpec((1, ATTN_WIDTH), lambda b, i: (0, 0))],
        out_specs=pl.BlockSpec((1, BLOCK, ATTN_WIDTH), lambda b, i: (b, i, 0)),
        out_shape=jax.ShapeDtypeStruct((bsz, seq, ATTN_WIDTH), BF16),
        scratch_shapes=[pltpu.VMEM((BLOCK, ATTN_WIDTH), F32)],
        compiler_params=_cparams("parallel", "parallel"),
        name="band_attn",
    )(sink, qr, kr, kr, kr, proj3, proj3, proj3, norm_w.reshape(1, ATTN_WIDTH))


def _conv_kernel(prev_ref, cur_ref, next_ref, w_ref, b_ref, o_ref):
    i = pl.program_id(1)
    last = pl.num_programs(1) - 1
    ts = cur_ref.shape[1]
    pad = CONV_WIDTH // 2
    prev = jnp.where(i > 0, prev_ref[0].astype(F32), 0.0)
    nxt = jnp.where(i < last, next_ref[0].astype(F32), 0.0)
    ext = jnp.concatenate([prev, cur_ref[0].astype(F32), nxt], axis=0)
    rows = ext.shape[0]
    w = w_ref[...]
    acc = jnp.broadcast_to(b_ref[...], (ts, ext.shape[1]))
    for j in range(CONV_WIDTH):
        d = j - pad
        shifted = ext if d == 0 else pltpu.roll(ext, (-d) % rows, 0)
        acc = acc + shifted[HALO_ROWS:HALO_ROWS + ts] * w[j:j + 1, :]
    o_ref[0] = _silu(acc).astype(o_ref.dtype)


def _conv(proj3, conv_w8, conv_b, ts=512, tc=1024):
    bsz, seq, _ = proj3.shape
    ts = min(ts, seq)
    col0 = XBC_OFF // tc
    hb = ts // HALO_ROWS
    nhalo = seq // HALO_ROWS
    return pl.pallas_call(
        _conv_kernel,
        grid=(bsz, seq // ts, XBC_WIDTH // tc),
        in_specs=[pl.BlockSpec((1, HALO_ROWS, tc), lambda b, i, j: (b, jnp.maximum(i * hb - 1, 0), col0 + j)),
                  pl.BlockSpec((1, ts, tc), lambda b, i, j: (b, i, col0 + j)),
                  pl.BlockSpec((1, HALO_ROWS, tc),
                               lambda b, i, j: (b, jnp.minimum((i + 1) * hb, nhalo - 1), col0 + j)),
                  pl.BlockSpec((8, tc), lambda b, i, j: (0, j)),
                  pl.BlockSpec((1, tc), lambda b, i, j: (0, j))],
        out_specs=pl.BlockSpec((1, ts, tc), lambda b, i, j: (b, i, j)),
        out_shape=jax.ShapeDtypeStruct((bsz, seq, XBC_WIDTH), BF16),
        compiler_params=_cparams("parallel", "parallel", "parallel"),
        name="dwconv_silu",
    )(proj3, proj3, proj3, conv_w8, conv_b.reshape(1, XBC_WIDTH))


def _bf16_pieces(v, parts):
    pieces = []
    rem = v
    for p in range(parts):
        piece = rem.astype(BF16)
        pieces.append(piece)
        if p + 1 < parts:
            rem = rem - piece.astype(F32)
    return pieces


def _ssd_chunk(xs_ref, b_ref, c_ref, dt_ref, par_ref, state_ref, emit, reverse):
    L = CHUNK

    @pl.when(pl.program_id(1) == 0)
    def _():
        state_ref[...] = jnp.zeros_like(state_ref)

    row = lax.broadcasted_iota(jnp.int32, (L, L), 0)
    col = lax.broadcasted_iota(jnp.int32, (L, L), 1)
    causal = (col >= row) if reverse else (col <= row)
    tri_b = jnp.where(causal, 1.0, 0.0).astype(BF16)

    xb = dt_ref[0] + par_ref[0:1, :]
    dt = jnp.maximum(xb, 0.0) + jnp.log1p(jnp.exp(-jnp.abs(xb)))
    adt = dt * (-jnp.exp(par_ref[1:2, :]))
    acum = sum(_dot_f32(tri_b, piece) for piece in _bf16_pieces(adt, 3))
    acum_t = acum.T
    edge = acum[0:1, :] if reverse else acum[L - 1:L, :]
    stack = jnp.concatenate([dt, jnp.exp(acum), jnp.exp(edge - acum)], axis=0)

    lane = lax.broadcasted_iota(jnp.int32, (L, LANES), 1)
    head_off = SSD_HEADS if reverse else 0
    for g in range(SSD_GROUPS):
        base = head_off + g * HEADS_PER_GROUP
        gsl = slice(g * GROUP_WIDTH, (g + 1) * GROUP_WIDTH)
        er = lax.broadcasted_iota(jnp.int32, (LANES, GROUP_WIDTH), 0)
        ec = lax.broadcasted_iota(jnp.int32, (LANES, GROUP_WIDTH), 1) // SSD_HEAD_DIM
        expand_b = jnp.where(er == ec + base, 1.0, 0.0).astype(BF16)
        ex = sum(_dot_f32(piece, expand_b) for piece in _bf16_pieces(stack, 2))
        dt_x, eac_x, ds_x = ex[0:L], ex[L:2 * L], ex[2 * L:3 * L]
        xs = xs_ref[0, :, gsl].astype(F32)
        xdt = xs * dt_x
        bm = b_ref[0, :, g * D_STATE:(g + 1) * D_STATE]
        cm = c_ref[0, :, g * D_STATE:(g + 1) * D_STATE]
        cb = lax.dot_general(cm, bm, (((1,), (1,)), ((), ())), preferred_element_type=F32)
        ys = []
        for pair in range(HEADS_PER_GROUP // 2):
            ms = []
            for r in (2 * pair, 2 * pair + 1):
                hl = base + r
                seg = acum[:, hl:hl + 1] - acum_t[hl:hl + 1, :]
                ms.append((cb * jnp.exp(jnp.where(causal, seg, -jnp.inf))).astype(BF16))
            xp = xdt[:, pair * LANES:(pair + 1) * LANES]
            rhs = jnp.concatenate([jnp.where(lane < SSD_HEAD_DIM, xp, 0.0),
                                   jnp.where(lane >= SSD_HEAD_DIM, xp, 0.0)], axis=0).astype(BF16)
            ys.append(jnp.dot(jnp.concatenate(ms, axis=1), rhs, preferred_element_type=F32))
        y_diag = jnp.concatenate(ys, axis=1)
        st = state_ref[g]
        y_off = jnp.dot(cm, st.astype(BF16), preferred_element_type=F32) * eac_x
        emit(g, gsl, xs, y_diag + y_off)
        new = lax.dot_general(bm, (xdt * ds_x).astype(BF16), (((0,), (0,)), ((), ())),
                              preferred_element_type=F32)
        cdec = eac_x[0:1, :] if reverse else eac_x[L - 1:L, :]
        state_ref[g] = st * cdec + new


def _ssd_bwd_kernel(xs_ref, b_ref, c_ref, dt_ref, par_ref, y_ref, state_ref):
    def emit(g, gsl, xs, y):
        y_ref[0, :, gsl] = y

    _ssd_chunk(xs_ref, b_ref, c_ref, dt_ref, par_ref, state_ref, emit, reverse=True)


def _ssd_fwd_gate_kernel(xs_ref, b_ref, c_ref, dt_ref, par_ref, yb_ref, zlo_ref, zhi_ref, d_ref, nw_ref,
                         o_ref, state_ref):
    half = SSD_GROUPS // 2

    def emit(g, gsl, xs, y):
        z_ref = zlo_ref if g < half else zhi_ref
        zsl = slice((g % half) * GROUP_WIDTH, (g % half + 1) * GROUP_WIDTH)
        y = y + yb_ref[0, :, gsl] + d_ref[g] * xs
        y = y * _silu(z_ref[0, :, zsl].astype(F32))
        y = y * lax.rsqrt(jnp.mean(y * y, axis=-1, keepdims=True) + RMS_EPS)
        o_ref[0, :, gsl] = (y * nw_ref[g]).astype(o_ref.dtype)

    _ssd_chunk(xs_ref, b_ref, c_ref, dt_ref, par_ref, state_ref, emit, reverse=False)


def _ssd_specs(nc, reverse):
    bc_w = SSD_GROUPS * D_STATE
    cidx = (lambda c: nc - 1 - c) if reverse else (lambda c: c)
    chunk = lambda width, col: pl.BlockSpec((1, CHUNK, width), lambda b, c: (b, cidx(c), col))
    specs = [chunk(SSD_WIDTH, 0), chunk(bc_w, SSD_WIDTH // bc_w), chunk(bc_w, SSD_WIDTH // bc_w + 1),
             chunk(DT_LANES, 0), pl.BlockSpec((8, LANES), lambda b, c: (0, 0))]
    return specs, chunk


def _ssd_bwd(xbc, dt_raw3, par):
    bsz, seq, _ = xbc.shape
    specs, chunk = _ssd_specs(seq // CHUNK, reverse=True)
    return pl.pallas_call(
        _ssd_bwd_kernel,
        grid=(bsz, seq // CHUNK),
        in_specs=specs,
        out_specs=chunk(SSD_WIDTH, 0),
        out_shape=jax.ShapeDtypeStruct((bsz, seq, SSD_WIDTH), F32),
        scratch_shapes=[pltpu.VMEM((SSD_GROUPS, D_STATE, GROUP_WIDTH), F32)],
        compiler_params=_cparams("parallel", "arbitrary"),
        name="ssd_bwd",
    )(xbc, xbc, xbc, dt_raw3, par)


def _ssd_fwd_gate(xbc, dt_raw3, par, y_b, proj3, d_x, norm_w):
    bsz, seq, _ = xbc.shape
    specs, chunk = _ssd_specs(seq // CHUNK, reverse=False)
    zw = SSD_WIDTH // 2
    per_group = pl.BlockSpec((SSD_GROUPS, 1, GROUP_WIDTH), lambda b, c: (0, 0, 0))
    return pl.pallas_call(
        _ssd_fwd_gate_kernel,
        grid=(bsz, seq // CHUNK),
        in_specs=specs + [chunk(SSD_WIDTH, 0), chunk(zw, Z_OFF // zw), chunk(zw, Z_OFF // zw + 1),
                          per_group, per_group],
        out_specs=chunk(SSD_WIDTH, 0),
        out_shape=jax.ShapeDtypeStruct((bsz, seq, SSD_WIDTH), BF16),
        scratch_shapes=[pltpu.VMEM((SSD_GROUPS, D_STATE, GROUP_WIDTH), F32)],
        compiler_params=_cparams("parallel", "arbitrary"),
        name="ssd_fwd_gate",
    )(xbc, xbc, xbc, dt_raw3, par, y_b, proj3, proj3, d_x, norm_w)


def _outproj_kernel(a_ref, s_ref, wa_ref, ws_ref, x_ref, gate_ref, o_ref):
    wa = wa_ref[...].astype(BF16)
    ws = ws_ref[...].astype(BF16)
    for rs in _row_chunks(a_ref.shape[0]):
        mix = _dot_f32(a_ref[rs, :], wa) + _dot_f32(s_ref[rs, :], ws)
        o_ref[rs, :] = ALPHA * x_ref[rs, :] + gate_ref[0] * mix


def _outproj(attn2, ssd2, w_out, x2, gate, seq, tm=2048, tn=256):
    m, ka = attn2.shape
    n = w_out.shape[1]
    tm = min(tm, seq)
    per_batch = seq // tm
    tile = pl.BlockSpec((tm, tn), lambda i, j: (i, j))
    return pl.pallas_call(
        _outproj_kernel,
        grid=(m // tm, n // tn),
        in_specs=[pl.BlockSpec((tm, ka), lambda i, j: (i, 0)),
                  pl.BlockSpec((tm, ka), lambda i, j: (i, 0)),
                  pl.BlockSpec((ka, tn), lambda i, j: (0, j)),
                  pl.BlockSpec((ka, tn), lambda i, j: (1, j)),
                  tile,
                  pl.BlockSpec((1, 1, tn), lambda i, j: (i // per_batch, 0, j))],
        out_specs=tile,
        out_shape=jax.ShapeDtypeStruct((m, n), F32),
        compiler_params=_cparams("parallel", "arbitrary"),
        name="out_proj",
    )(attn2, ssd2, w_out, w_out, x2, gate)


def _ln_body(v_ref, g_ref, b_ref):
    v = v_ref[...]
    mu = jnp.mean(v, axis=-1, keepdims=True)
    vc = v - mu
    var = jnp.mean(vc * vc, axis=-1, keepdims=True)
    return vc * lax.rsqrt(var + LN_EPS) * g_ref[...] + b_ref[...]


def _ln_mod_kernel(v_ref, g_ref, b_ref, sc_ref, sh_ref, o_ref, h_ref):
    out = _ln_body(v_ref, g_ref, b_ref)
    o_ref[...] = out
    h_ref[...] = (out * (1.0 + sc_ref[0]) + sh_ref[0]).astype(h_ref.dtype)


def _ln_kernel(v_ref, g_ref, b_ref, o_ref):
    o_ref[...] = _ln_body(v_ref, g_ref, b_ref)


def _deepnorm(v, ln_g, ln_b, seq, mod=None, tm=256):
    m, d = v.shape
    tm = min(tm, seq)
    per_batch = seq // tm
    rows = pl.BlockSpec((tm, d), lambda i: (i, 0))
    per_b = pl.BlockSpec((1, 1, d), lambda i: (i // per_batch, 0, 0))
    vec = pl.BlockSpec((1, d), lambda i: (0, 0))
    args = [v, ln_g.reshape(1, d), ln_b.reshape(1, d)]
    in_specs = [rows, vec, vec]
    if mod is None:
        return pl.pallas_call(
            _ln_kernel, grid=(m // tm,), in_specs=in_specs, out_specs=rows,
            out_shape=jax.ShapeDtypeStruct((m, d), F32),
            compiler_params=_cparams("parallel"), name="deepnorm_ln",
        )(*args)
    return pl.pallas_call(
        _ln_mod_kernel, grid=(m // tm,), in_specs=in_specs + [per_b, per_b],
        out_specs=[rows, rows],
        out_shape=[jax.ShapeDtypeStruct((m, d), F32), jax.ShapeDtypeStruct((m, d), BF16)],
        compiler_params=_cparams("parallel"), name="deepnorm_ln_mod",
    )(*args, *mod)


def _ffn_up_kernel(h_ref, wg_ref, wu_ref, wd_ref, o_ref, wdo_ref):
    tn = wg_ref.shape[1]
    w = jnp.concatenate([wg_ref[...].astype(BF16), wu_ref[...].astype(BF16)], axis=1)
    for rs in _row_chunks(h_ref.shape[0]):
        r = _dot_f32(h_ref[rs, :], w)
        o_ref[rs, :] = (_silu(r[:, :tn]) * r[:, tn:]).astype(o_ref.dtype)
    wdo_ref[...] = wd_ref[...].astype(wdo_ref.dtype)


def _ffn_up(h2, w_gate, w_up, w_down, tm=2048, tn=256):
    m, d = h2.shape
    n = w_gate.shape[1]
    tm = min(tm, m)
    nj = n // tn
    steps = (m // tm) * nj
    slab, rem = divmod(w_down.shape[0], steps)
    assert rem == 0 and slab % HALO_ROWS == 0, (w_down.shape, steps)
    wspec = pl.BlockSpec((d, tn), lambda i, j: (0, j))
    slab_spec = pl.BlockSpec((slab, w_down.shape[1]), lambda i, j: (i * nj + j, 0))
    return pl.pallas_call(
        _ffn_up_kernel,
        grid=(m // tm, nj),
        in_specs=[_resident((tm, d), lambda i, j: (i, 0)), wspec, wspec, slab_spec],
        out_specs=[pl.BlockSpec((tm, tn), lambda i, j: (i, j)), slab_spec],
        out_shape=[jax.ShapeDtypeStruct((m, n), BF16),
                   jax.ShapeDtypeStruct(w_down.shape, BF16)],
        compiler_params=_cparams("parallel", "arbitrary"),
        name="ffn_up",
    )(h2, w_gate, w_up, w_down)


def _ffn_down_kernel(a_ref, w_ref, x_ref, gate_ref, o_ref):
    o_ref[...] = ALPHA * x_ref[...] + gate_ref[0] * _dot_f32(a_ref[...], w_ref[...])


def _ffn_down(act, w_down, x1, gate, seq, tm=512, tn=512):
    m, k = act.shape
    n = w_down.shape[1]
    tm = min(tm, seq)
    per_batch = seq // tm
    tile = pl.BlockSpec((tm, tn), lambda i, j: (i, j))
    return pl.pallas_call(
        _ffn_down_kernel,
        grid=(m // tm, n // tn),
        in_specs=[pl.BlockSpec((tm, k), lambda i, j: (i, 0)),
                  pl.BlockSpec((k, tn), lambda i, j: (0, j)),
                  tile,
                  pl.BlockSpec((1, 1, tn), lambda i, j: (i // per_batch, 0, j))],
        out_specs=tile,
        out_shape=jax.ShapeDtypeStruct((m, n), F32),
        compiler_params=_cparams("parallel", "arbitrary"),
        name="ffn_down",
    )(act, w_down, x1, gate)


def _rope_tables(positions):
    half = ROPE_DIM // 2
    inv_freq = ROPE_THETA ** (-jnp.arange(0, ROPE_DIM, 2, dtype=F32) / ROPE_DIM)
    ang = positions.astype(F32)[..., None] * inv_freq
    cos, sin = jnp.cos(ang), jnp.sin(ang)
    rest = HEAD_DIM - ROPE_DIM
    shp = ang.shape[:-1]
    cos_t = jnp.concatenate([cos, cos, jnp.ones(shp + (rest,), F32)], axis=-1)
    s1_t = jnp.concatenate([-sin, jnp.zeros(shp + (HEAD_DIM - half,), F32)], axis=-1)
    s2_t = jnp.concatenate([jnp.zeros(shp + (half,), F32), sin, jnp.zeros(shp + (rest,), F32)], axis=-1)
    return cos_t, s1_t, s2_t


def kernel(x, c, positions, w_ada, b_ada, w_in, conv_w, conv_b, attn_sink, a_log_fwd, a_log_bwd,
           dt_bias_fwd, dt_bias_bwd, ssd_d, ssd_norm_w, attn_norm_w, w_out, ln1_g, ln1_b,
           w_gate, w_up, w_down, ln2_g, ln2_b):
    assert w_ada.shape[0] == 1, "single-layer block"
    bsz, seq, d = x.shape
    m = bsz * seq
    x2 = x.reshape(m, d)

    b_ada2 = b_ada[0].reshape(1, 6 * d)
    mod_head, cond_t = _ada_head(c, w_ada[0], b_ada2, 2 * d)
    sh1, sc1 = [t.reshape(bsz, 1, d) for t in jnp.split(mod_head, 2, axis=-1)]

    proj, dt_raw, mod_tail = _inproj(_modulate(x2, sc1, sh1, seq), jnp.swapaxes(w_in[0], 0, 1),
                                     cond_t, w_ada[0], b_ada2, 2 * d)
    g1, sh2, sc2, g2 = [t.reshape(bsz, 1, d) for t in jnp.split(mod_tail, 4, axis=-1)]
    proj3 = proj.reshape(bsz, seq, MAIN_WIDTH)

    qr, kr = _rope(proj3, *_rope_tables(positions))
    attn = _attention(qr, kr, proj3, attn_sink[0], attn_norm_w[0])

    conv_w8 = jnp.pad(conv_w[0], ((0, 8 - CONV_WIDTH), (0, 0)))
    xbc = _conv(proj3, conv_w8, conv_b[0])
    zeros = jnp.zeros((DT_LANES - 2 * SSD_HEADS,), F32)
    par = jnp.zeros((8, LANES), F32)
    par = par.at[0].set(jnp.concatenate([dt_bias_fwd[0], dt_bias_bwd[0], zeros]))
    par = par.at[1].set(jnp.concatenate([a_log_fwd[0], a_log_bwd[0], zeros]))
    dt_raw3 = dt_raw.reshape(bsz, seq, DT_LANES)
    y_b = _ssd_bwd(xbc, dt_raw3, par)
    d_x = jnp.repeat(ssd_d[0], SSD_HEAD_DIM).reshape(SSD_GROUPS, 1, GROUP_WIDTH)
    ssd = _ssd_fwd_gate(xbc, dt_raw3, par, y_b, proj3, d_x,
                        ssd_norm_w[0].reshape(SSD_GROUPS, 1, GROUP_WIDTH))

    v1 = _outproj(attn.reshape(m, ATTN_WIDTH), ssd.reshape(m, SSD_WIDTH), w_out[0], x2, g1, seq)
    x1, h2 = _deepnorm(v1, ln1_g[0], ln1_b[0], seq, mod=(sc2, sh2))

    act, w_down_b = _ffn_up(h2, w_gate[0], w_up[0], w_down[0])
    v2 = _ffn_down(act, w_down_b, x1, g2, seq)
    out = _deepnorm(v2, ln2_g[0], ln2_b[0], seq)
    return out.reshape(bsz, seq, d)
```

```python
import functools

import jax
import jax.numpy as jnp
from jax import lax
from jax.experimental import pallas as pl
from jax.experimental.pallas import tpu as pltpu

F32 = jnp.float32
BF16 = jnp.bfloat16

D_MODEL = 4096
HEAD_DIM = 128
ATTN_WIDTH = D_MODEL // 2
N_Q_HEADS = ATTN_WIDTH // HEAD_DIM
N_KV_HEADS = N_Q_HEADS // 4
Q_PER_KV = N_Q_HEADS // N_KV_HEADS
KV_WIDTH = N_KV_HEADS * HEAD_DIM
WINDOW = 128
BLOCK = 128
ROPE_DIM = HEAD_DIM // 4
ROPE_THETA = 500000.0
SSD_WIDTH = D_MODEL // 2
SSD_HEAD_DIM = 64
SSD_HEADS = SSD_WIDTH // SSD_HEAD_DIM
SSD_GROUPS = 4
HEADS_PER_GROUP = SSD_HEADS // SSD_GROUPS
GROUP_WIDTH = SSD_WIDTH // SSD_GROUPS
D_STATE = 128
CONV_WIDTH = 5
CHUNK = 128
XBC_WIDTH = SSD_WIDTH + 2 * SSD_GROUPS * D_STATE
MAIN_WIDTH = ATTN_WIDTH + 2 * KV_WIDTH + SSD_WIDTH + XBC_WIDTH
D_FF = -(-8 * D_MODEL // (3 * 256)) * 256
LN_EPS = 1e-5
RMS_EPS = 1e-6
ALPHA = 2.0 ** 0.25
LOG2_E = 1.4426950408889634

Q_OFF = 0
K_OFF = ATTN_WIDTH
V_OFF = K_OFF + KV_WIDTH
Z_OFF = V_OFF + KV_WIDTH
XBC_OFF = Z_OFF + SSD_WIDTH
assert MAIN_WIDTH == XBC_OFF + XBC_WIDTH

V7X_VMEM_BYTES = 64 * 1024 * 1024
VMEM_LIMIT = V7X_VMEM_BYTES * 7 // 8
LANES = 128
DT_LANES = LANES
HALO_ROWS = 16
DOT_ROWS = 1024


def _cparams(*sem):
    return pltpu.CompilerParams(dimension_semantics=sem, vmem_limit_bytes=VMEM_LIMIT)


def _silu(v):
    return v / (1.0 + jnp.exp(-v))


def _dot_f32(a, b):
    return jnp.dot(a, b, preferred_element_type=F32)


def _cond_matvec(cond_t, w):
    rows = [jnp.sum(w * cond_t[:, b:b + 1], axis=0, keepdims=True) for b in range(cond_t.shape[1])]
    return jnp.concatenate(rows, axis=0)


def _ada_head_kernel(ct_ref, w_ref, b_ref, o_ref, cond_ref):
    @pl.when(pl.program_id(0) == 0)
    def _():
        o_ref[...] = jnp.broadcast_to(b_ref[...], o_ref.shape)

    cond = _silu(ct_ref[...])
    cond_ref[...] = cond
    o_ref[...] += _cond_matvec(cond, w_ref[...])


def _ada_head(c, w_ada, b_ada2, n_head, tk=256):
    bsz, d = c.shape
    return pl.pallas_call(
        _ada_head_kernel,
        grid=(d // tk,),
        in_specs=[pl.BlockSpec((tk, bsz), lambda k: (k, 0)),
                  pl.BlockSpec((tk, n_head), lambda k: (k, 0)),
                  pl.BlockSpec((1, n_head), lambda k: (0, 0))],
        out_specs=[pl.BlockSpec((bsz, n_head), lambda k: (0, 0)),
                   pl.BlockSpec((tk, bsz), lambda k: (k, 0))],
        out_shape=[jax.ShapeDtypeStruct((bsz, n_head), F32),
                   jax.ShapeDtypeStruct((d, bsz), F32)],
        compiler_params=_cparams("arbitrary"),
        name="ada_head",
    )(c.T, w_ada, b_ada2)


def _modulate_kernel(x_ref, sc_ref, sh_ref, h_ref):
    h_ref[...] = (x_ref[...] * (1.0 + sc_ref[0]) + sh_ref[0]).astype(h_ref.dtype)


def _modulate(x2, scale, shift, seq, tm=512):
    m, d = x2.shape
    tm = min(tm, seq)
    per_batch = seq // tm
    per_b = pl.BlockSpec((1, 1, d), lambda i: (i // per_batch, 0, 0))
    return pl.pallas_call(
        _modulate_kernel,
        grid=(m // tm,),
        in_specs=[pl.BlockSpec((tm, d), lambda i: (i, 0)), per_b, per_b],
        out_specs=pl.BlockSpec((tm, d), lambda i: (i, 0)),
        out_shape=jax.ShapeDtypeStruct((m, d), BF16),
        compiler_params=_cparams("parallel"),
        name="modulate",
    )(x2, scale, shift)


def _resident(block_shape, index_map):
    return pl.BlockSpec(block_shape, index_map, pipeline_mode=pl.Buffered(1))


def _row_chunks(rows):
    step = min(rows, DOT_ROWS)
    return [slice(r, r + step) for r in range(0, rows, step)]


def _dot_nt(a, bt):
    return lax.dot_general(a, bt, (((1,), (1,)), ((), ())), preferred_element_type=F32)


def _inproj_kernel(h_ref, wt_ref, wdt_ref, cond_ref, wada_ref, bada_ref, o_ref, odt_ref, mod_ref):
    @pl.when(pl.program_id(1) == 0)
    def _():
        wdt = wdt_ref[...].astype(BF16)
        for rs in _row_chunks(h_ref.shape[0]):
            r = _dot_nt(h_ref[rs, :], wdt)
            odt_ref[rs, :] = jnp.concatenate([r, jnp.zeros((r.shape[0], DT_LANES - r.shape[1]), F32)], axis=1)

    wt = wt_ref[...].astype(BF16)
    for rs in _row_chunks(h_ref.shape[0]):
        o_ref[rs, :] = _dot_nt(h_ref[rs, :], wt).astype(o_ref.dtype)
    mod_ref[...] = _cond_matvec(cond_ref[...], wada_ref[...]) + bada_ref[...]


def _inproj(h, w_in_t, cond_t, w_ada, b_ada2, n_head, tm=2048, tn=256):
    m, d = h.shape
    tm = min(tm, m)
    nj = MAIN_WIDTH // tn
    n_dt = w_in_t.shape[0] - MAIN_WIDTH
    assert MAIN_WIDTH % n_dt == 0 and n_dt % 8 == 0
    bsz = cond_t.shape[1]
    n_tail = w_ada.shape[1] - n_head
    slab, rem = divmod(n_tail, (m // tm) * nj)
    assert rem == 0 and slab % LANES == 0 and n_head % slab == 0, (n_tail, m // tm, nj)
    slab_col = lambda i, j: (0, n_head // slab + i * nj + j)
    return pl.pallas_call(
        _inproj_kernel,
        grid=(m // tm, nj),
        in_specs=[_resident((tm, d), lambda i, j: (i, 0)),
                  pl.BlockSpec((tn, d), lambda i, j: (j, 0)),
                  pl.BlockSpec((n_dt, d), lambda i, j: (MAIN_WIDTH // n_dt, 0)),
                  pl.BlockSpec((d, bsz), lambda i, j: (0, 0)),
                  pl.BlockSpec((d, slab), slab_col),
                  pl.BlockSpec((1, slab), slab_col)],
        out_specs=[pl.BlockSpec((tm, tn), lambda i, j: (i, j)),
                   pl.BlockSpec((tm, DT_LANES), lambda i, j: (i, 0)),
                   pl.BlockSpec((bsz, slab), lambda i, j: (0, i * nj + j))],
        out_shape=[jax.ShapeDtypeStruct((m, MAIN_WIDTH), BF16),
                   jax.ShapeDtypeStruct((m, DT_LANES), F32),
                   jax.ShapeDtypeStruct((bsz, n_tail), F32)],
        compiler_params=_cparams("parallel", "arbitrary"),
        name="in_proj",
    )(h, w_in_t, w_in_t, cond_t, w_ada, b_ada2)


def _rope_kernel(q_ref, k_ref, c_ref, s1_ref, s2_ref, qo_ref, ko_ref):
    cos, s1, s2 = c_ref[0], s1_ref[0], s2_ref[0]
    half = ROPE_DIM // 2

    def rot(t):
        t = t.astype(F32)
        return (t * cos + pltpu.roll(t, HEAD_DIM - half, 1) * s1 + pltpu.roll(t, half, 1) * s2)

    for h in range(N_Q_HEADS):
        sl = slice(h * HEAD_DIM, (h + 1) * HEAD_DIM)
        qo_ref[0, :, sl] = rot(q_ref[0, :, sl]).astype(qo_ref.dtype)
    for h in range(N_KV_HEADS):
        sl = slice(h * HEAD_DIM, (h + 1) * HEAD_DIM)
        ko_ref[0, :, sl] = rot(k_ref[0, :, sl]).astype(ko_ref.dtype)


def _rope(proj3, cos_t, s1_t, s2_t, ts=256):
    bsz, seq, _ = proj3.shape
    ts = min(ts, seq)
    tab = pl.BlockSpec((1, ts, HEAD_DIM), lambda b, i: (b, i, 0))
    return pl.pallas_call(
        _rope_kernel,
        grid=(bsz, seq // ts),
        in_specs=[pl.BlockSpec((1, ts, ATTN_WIDTH), lambda b, i: (b, i, Q_OFF // ATTN_WIDTH)),
                  pl.BlockSpec((1, ts, KV_WIDTH), lambda b, i: (b, i, K_OFF // KV_WIDTH)),
                  tab, tab, tab],
        out_specs=[pl.BlockSpec((1, ts, ATTN_WIDTH), lambda b, i: (b, i, 0)),
                   pl.BlockSpec((1, ts, KV_WIDTH), lambda b, i: (b, i, 0))],
        out_shape=[jax.ShapeDtypeStruct((bsz, seq, ATTN_WIDTH), BF16),
                   jax.ShapeDtypeStruct((bsz, seq, KV_WIDTH), BF16)],
        compiler_params=_cparams("parallel", "parallel"),
        name="rope",
    )(proj3, proj3, cos_t, s1_t, s2_t)


def _attn_kernel(sink_ref, q_ref, kp_ref, kc_ref, kn_ref, vp_ref, vc_ref, vn_ref, nw_ref,
                 o_ref, acc_ref, *, seq):
    i = pl.program_id(1)
    qi = lax.broadcasted_iota(jnp.int32, (BLOCK, 3 * BLOCK), 0)
    sj = lax.broadcasted_iota(jnp.int32, (BLOCK, 3 * BLOCK), 1)
    kpos = i * BLOCK + sj - BLOCK
    keep = jnp.where(jnp.abs(sj - BLOCK - qi) <= WINDOW,
                     jnp.where(kpos >= 0, jnp.where(kpos < seq, 1, 0), 0), 0)
    bias = jnp.where(keep == 1, 0.0, -jnp.inf).astype(F32)
    scale2 = HEAD_DIM ** -0.5 * LOG2_E
    ssq = jnp.zeros((BLOCK, 1), F32)
    for kv in range(N_KV_HEADS):
        ksl = slice(kv * HEAD_DIM, (kv + 1) * HEAD_DIM)
        kwin = jnp.concatenate([kp_ref[0, :, ksl], kc_ref[0, :, ksl], kn_ref[0, :, ksl]], axis=0)
        vwin = jnp.concatenate([vp_ref[0, :, ksl], vc_ref[0, :, ksl], vn_ref[0, :, ksl]], axis=0)
        q4 = jnp.concatenate(
            [q_ref[0, :, (kv * Q_PER_KV + g) * HEAD_DIM:(kv * Q_PER_KV + g + 1) * HEAD_DIM]
             for g in range(Q_PER_KV)], axis=0)
        s = lax.dot_general(q4, kwin, (((1,), (1,)), ((), ())), preferred_element_type=F32)
        probs, denoms = [], []
        for g in range(Q_PER_KV):
            sink2 = sink_ref[kv * Q_PER_KV + g] * LOG2_E
            sg = s[g * BLOCK:(g + 1) * BLOCK] * scale2 + bias
            m = jnp.maximum(jnp.max(sg, axis=-1, keepdims=True), sink2)
            p = jnp.exp2(sg - m)
            denoms.append(jnp.sum(p, axis=-1, keepdims=True) + jnp.exp2(sink2 - m))
            probs.append(p.astype(BF16))
        o = jnp.dot(jnp.concatenate(probs, axis=0), vwin, preferred_element_type=F32)
        for g in range(Q_PER_KV):
            og = o[g * BLOCK:(g + 1) * BLOCK] / denoms[g]
            h = kv * Q_PER_KV + g
            acc_ref[:, h * HEAD_DIM:(h + 1) * HEAD_DIM] = og
            ssq = ssq + jnp.sum(og * og, axis=-1, keepdims=True)
    inv = lax.rsqrt(ssq / ATTN_WIDTH + RMS_EPS)
    o_ref[0] = (acc_ref[...] * inv * nw_ref[...]).astype(o_ref.dtype)


def _attention(qr, kr, proj3, sink, norm_w):
    bsz, seq, _ = qr.shape
    nb = seq // BLOCK
    vcol = V_OFF // KV_WIDTH

    def kv_specs(col):
        return [pl.BlockSpec((1, BLOCK, KV_WIDTH), lambda b, i: (b, jnp.maximum(i - 1, 0), col)),
                pl.BlockSpec((1, BLOCK, KV_WIDTH), lambda b, i: (b, i, col)),
                pl.BlockSpec((1, BLOCK, KV_WIDTH), lambda b, i: (b, jnp.minimum(i + 1, nb - 1), col))]

    return pl.pallas_call(
        functools.partial(_attn_kernel, seq=seq),
        grid=(bsz, nb),
        in_specs=[pl.BlockSpec(memory_space=pltpu.SMEM),
                  pl.BlockSpec((1, BLOCK, ATTN_WIDTH), lambda b, i: (b, i, 0))]
                 + kv_specs(0) + kv_specs(vcol)
                 + [pl.BlockSpec((1, ATTN_WIDTH), lambda b, i: (0, 0))],
        out_specs=pl.BlockSpec((1, BLOCK, ATTN_WIDTH), lambda b, i: (b, i, 0)),
        out_shape=jax.ShapeDtypeStruct((bsz, seq, ATTN_WIDTH), BF16),
        scratch_shapes=[pltpu.VMEM((BLOCK, ATTN_WIDTH), F32)],
        compiler_params=_cparams("parallel", "parallel"),
        name="band_attn",
    )(sink, qr, kr, kr, kr, proj3, proj3, proj3, norm_w.reshape(1, ATTN_WIDTH))


def _conv_kernel(prev_ref, cur_ref, next_ref, w_ref, b_ref, o_ref):
    i = pl.program_id(1)
    last = pl.num_programs(1) - 1
    ts = cur_ref.shape[1]
    pad = CONV_WIDTH // 2
    prev = jnp.where(i > 0, prev_ref[0].astype(F32), 0.0)
    nxt = jnp.where(i < last, next_ref[0].astype(F32), 0.0)
    ext = jnp.concatenate([prev, cur_ref[0].astype(F32), nxt], axis=0)
    rows = ext.shape[0]
    w = w_ref[...]
    acc = jnp.broadcast_to(b_ref[...], (ts, ext.shape[1]))
    for j in range(CONV_WIDTH):
        d = j - pad
        shifted = ext if d == 0 else pltpu.roll(ext, (-d) % rows, 0)
        acc = acc + shifted[HALO_ROWS:HALO_ROWS + ts] * w[j:j + 1, :]
    o_ref[0] = _silu(acc).astype(o_ref.dtype)


def _conv(proj3, conv_w8, conv_b, ts=512, tc=1024):
    bsz, seq, _ = proj3.shape
    ts = min(ts, seq)
    col0 = XBC_OFF // tc
    hb = ts // HALO_ROWS
    nhalo = seq // HALO_ROWS
    return pl.pallas_call(
        _conv_kernel,
        grid=(bsz, seq // ts, XBC_WIDTH // tc),
        in_specs=[pl.BlockSpec((1, HALO_ROWS, tc), lambda b, i, j: (b, jnp.maximum(i * hb - 1, 0), col0 + j)),
                  pl.BlockSpec((1, ts, tc), lambda b, i, j: (b, i, col0 + j)),
                  pl.BlockSpec((1, HALO_ROWS, tc),
                               lambda b, i, j: (b, jnp.minimum((i + 1) * hb, nhalo - 1), col0 + j)),
                  pl.BlockSpec((8, tc), lambda b, i, j: (0, j)),
                  pl.BlockSpec((1, tc), lambda b, i, j: (0, j))],
        out_specs=pl.BlockSpec((1, ts, tc), lambda b, i, j: (b, i, j)),
        out_shape=jax.ShapeDtypeStruct((bsz, seq, XBC_WIDTH), BF16),
        compiler_params=_cparams("parallel", "parallel", "parallel"),
        name="dwconv_silu",
    )(proj3, proj3, proj3, conv_w8, conv_b.reshape(1, XBC_WIDTH))


def _bf16_pieces(v, parts):
    pieces = []
    rem = v
    for p in range(parts):
        piece = rem.astype(BF16)
        pieces.append(piece)
        if p + 1 < parts:
            rem = rem - piece.astype(F32)
    return pieces


def _ssd_chunk(xs_ref, b_ref, c_ref, dt_ref, par_ref, state_ref, emit, reverse):
    L = CHUNK

    @pl.when(pl.program_id(1) == 0)
    def _():
        state_ref[...] = jnp.zeros_like(state_ref)

    row = lax.broadcasted_iota(jnp.int32, (L, L), 0)
    col = lax.broadcasted_iota(jnp.int32, (L, L), 1)
    causal = (col >= row) if reverse else (col <= row)
    tri_b = jnp.where(causal, 1.0, 0.0).astype(BF16)

    xb = dt_ref[0] + par_ref[0:1, :]
    dt = jnp.maximum(xb, 0.0) + jnp.log1p(jnp.exp(-jnp.abs(xb)))
    adt = dt * (-jnp.exp(par_ref[1:2, :]))
    acum = sum(_dot_f32(tri_b, piece) for piece in _bf16_pieces(adt, 3))
    acum_t = acum.T
    edge = acum[0:1, :] if reverse else acum[L - 1:L, :]
    stack = jnp.concatenate([dt, jnp.exp(acum), jnp.exp(edge - acum)], axis=0)

    lane = lax.broadcasted_iota(jnp.int32, (L, LANES), 1)
    head_off = SSD_HEADS if reverse else 0
    for g in range(SSD_GROUPS):
        base = head_off + g * HEADS_PER_GROUP
        gsl = slice(g * GROUP_WIDTH, (g + 1) * GROUP_WIDTH)
        er = lax.broadcasted_iota(jnp.int32, (LANES, GROUP_WIDTH), 0)
        ec = lax.broadcasted_iota(jnp.int32, (LANES, GROUP_WIDTH), 1) // SSD_HEAD_DIM
        expand_b = jnp.where(er == ec + base, 1.0, 0.0).astype(BF16)
        ex = sum(_dot_f32(piece, expand_b) for piece in _bf16_pieces(stack, 2))
        dt_x, eac_x, ds_x = ex[0:L], ex[L:2 * L], ex[2 * L:3 * L]
        xs = xs_ref[0, :, gsl].astype(F32)
        xdt = xs * dt_x
        bm = b_ref[0, :, g * D_STATE:(g + 1) * D_STATE]
        cm = c_ref[0, :, g * D_STATE:(g + 1) * D_STATE]
        cb = lax.dot_general(cm, bm, (((1,), (1,)), ((), ())), preferred_element_type=F32)
        ys = []
        for pair in range(HEADS_PER_GROUP // 2):
            ms = []
            for r in (2 * pair, 2 * pair + 1):
                hl = base + r
                seg = acum[:, hl:hl + 1] - acum_t[hl:hl + 1, :]
                ms.append((cb * jnp.exp(jnp.where(causal, seg, -jnp.inf))).astype(BF16))
            xp = xdt[:, pair * LANES:(pair + 1) * LANES]
            rhs = jnp.concatenate([jnp.where(lane < SSD_HEAD_DIM, xp, 0.0),
                                   jnp.where(lane >= SSD_HEAD_DIM, xp, 0.0)], axis=0).astype(BF16)
            ys.append(jnp.dot(jnp.concatenate(ms, axis=1), rhs, preferred_element_type=F32))
        y_diag = jnp.concatenate(ys, axis=1)
        st = state_ref[g]
        y_off = jnp.dot(cm, st.astype(BF16), preferred_element_type=F32) * eac_x
        emit(g, gsl, xs, y_diag + y_off)
        new = lax.dot_general(bm, (xdt * ds_x).astype(BF16), (((0,), (0,)), ((), ())),
                              preferred_element_type=F32)
        cdec = eac_x[0:1, :] if reverse else eac_x[L - 1:L, :]
        state_ref[g] = st * cdec + new


def _ssd_bwd_kernel(xs_ref, b_ref, c_ref, dt_ref, par_ref, y_ref, state_ref):
    def emit(g, gsl, xs, y):
        y_ref[0, :, gsl] = y

    _ssd_chunk(xs_ref, b_ref, c_ref, dt_ref, par_ref, state_ref, emit, reverse=True)


def _ssd_fwd_gate_kernel(xs_ref, b_ref, c_ref, dt_ref, par_ref, yb_ref, zlo_ref, zhi_ref, d_ref, nw_ref,
                         o_ref, state_ref):
    half = SSD_GROUPS // 2

    def emit(g, gsl, xs, y):
        z_ref = zlo_ref if g < half else zhi_ref
        zsl = slice((g % half) * GROUP_WIDTH, (g % half + 1) * GROUP_WIDTH)
        y = y + yb_ref[0, :, gsl] + d_ref[g] * xs
        y = y * _silu(z_ref[0, :, zsl].astype(F32))
        y = y * lax.rsqrt(jnp.mean(y * y, axis=-1, keepdims=True) + RMS_EPS)
        o_ref[0, :, gsl] = (y * nw_ref[g]).astype(o_ref.dtype)

    _ssd_chunk(xs_ref, b_ref, c_ref, dt_ref, par_ref, state_ref, emit, reverse=False)


def _ssd_specs(nc, reverse):
    bc_w = SSD_GROUPS * D_STATE
    cidx = (lambda c: nc - 1 - c) if reverse else (lambda c: c)
    chunk = lambda width, col: pl.BlockSpec((1, CHUNK, width), lambda b, c: (b, cidx(c), col))
    specs = [chunk(SSD_WIDTH, 0), chunk(bc_w, SSD_WIDTH // bc_w), chunk(bc_w, SSD_WIDTH // bc_w + 1),
             chunk(DT_LANES, 0), pl.BlockSpec((8, LANES), lambda b, c: (0, 0))]
    return specs, chunk


def _ssd_bwd(xbc, dt_raw3, par):
    bsz, seq, _ = xbc.shape
    specs, chunk = _ssd_specs(seq // CHUNK, reverse=True)
    return pl.pallas_call(
        _ssd_bwd_kernel,
        grid=(bsz, seq // CHUNK),
        in_specs=specs,
        out_specs=chunk(SSD_WIDTH, 0),
        out_shape=jax.ShapeDtypeStruct((bsz, seq, SSD_WIDTH), F32),
        scratch_shapes=[pltpu.VMEM((SSD_GROUPS, D_STATE, GROUP_WIDTH), F32)],
        compiler_params=_cparams("parallel", "arbitrary"),
        name="ssd_bwd",
    )(xbc, xbc, xbc, dt_raw3, par)


def _ssd_fwd_gate(xbc, dt_raw3, par, y_b, proj3, d_x, norm_w):
    bsz, seq, _ = xbc.shape
    specs, chunk = _ssd_specs(seq // CHUNK, reverse=False)
    zw = SSD_WIDTH // 2
    per_group = pl.BlockSpec((SSD_GROUPS, 1, GROUP_WIDTH), lambda b, c: (0, 0, 0))
    return pl.pallas_call(
        _ssd_fwd_gate_kernel,
        grid=(bsz, seq // CHUNK),
        in_specs=specs + [chunk(SSD_WIDTH, 0), chunk(zw, Z_OFF // zw), chunk(zw, Z_OFF // zw + 1),
                          per_group, per_group],
        out_specs=chunk(SSD_WIDTH, 0),
        out_shape=jax.ShapeDtypeStruct((bsz, seq, SSD_WIDTH), BF16),
        scratch_shapes=[pltpu.VMEM((SSD_GROUPS, D_STATE, GROUP_WIDTH), F32)],
        compiler_params=_cparams("parallel", "arbitrary"),
        name="ssd_fwd_gate",
    )(xbc, xbc, xbc, dt_raw3, par, y_b, proj3, proj3, d_x, norm_w)


def _outproj_kernel(a_ref, s_ref, wa_ref, ws_ref, x_ref, gate_ref, o_ref):
    wa = wa_ref[...].astype(BF16)
    ws = ws_ref[...].astype(BF16)
    for rs in _row_chunks(a_ref.shape[0]):
        mix = _dot_f32(a_ref[rs, :], wa) + _dot_f32(s_ref[rs, :], ws)
        o_ref[rs, :] = ALPHA * x_ref[rs, :] + gate_ref[0] * mix


def _outproj(attn2, ssd2, w_out, x2, gate, seq, tm=2048, tn=256):
    m, ka = attn2.shape
    n = w_out.shape[1]
    tm = min(tm, seq)
    per_batch = seq // tm
    tile = pl.BlockSpec((tm, tn), lambda i, j: (i, j))
    return pl.pallas_call(
        _outproj_kernel,
        grid=(m // tm, n // tn),
        in_specs=[pl.BlockSpec((tm, ka), lambda i, j: (i, 0)),
                  pl.BlockSpec((tm, ka), lambda i, j: (i, 0)),
                  pl.BlockSpec((ka, tn), lambda i, j: (0, j)),
                  pl.BlockSpec((ka, tn), lambda i, j: (1, j)),
                  tile,
                  pl.BlockSpec((1, 1, tn), lambda i, j: (i // per_batch, 0, j))],
        out_specs=tile,
        out_shape=jax.ShapeDtypeStruct((m, n), F32),
        compiler_params=_cparams("parallel", "arbitrary"),
        name="out_proj",
    )(attn2, ssd2, w_out, w_out, x2, gate)


def _ln_body(v_ref, g_ref, b_ref):
    v = v_ref[...]
    mu = jnp.mean(v, axis=-1, keepdims=True)
    vc = v - mu
    var = jnp.mean(vc * vc, axis=-1, keepdims=True)
    return vc * lax.rsqrt(var + LN_EPS) * g_ref[...] + b_ref[...]


def _ln_mod_kernel(v_ref, g_ref, b_ref, sc_ref, sh_ref, o_ref, h_ref):
    out = _ln_body(v_ref, g_ref, b_ref)
    o_ref[...] = out
    h_ref[...] = (out * (1.0 + sc_ref[0]) + sh_ref[0]).astype(h_ref.dtype)


def _ln_kernel(v_ref, g_ref, b_ref, o_ref):
    o_ref[...] = _ln_body(v_ref, g_ref, b_ref)


def _deepnorm(v, ln_g, ln_b, seq, mod=None, tm=256):
    m, d = v.shape
    tm = min(tm, seq)
    per_batch = seq // tm
    rows = pl.BlockSpec((tm, d), lambda i: (i, 0))
    per_b = pl.BlockSpec((1, 1, d), lambda i: (i // per_batch, 0, 0))
    vec = pl.BlockSpec((1, d), lambda i: (0, 0))
    args = [v, ln_g.reshape(1, d), ln_b.reshape(1, d)]
    in_specs = [rows, vec, vec]
    if mod is None:
        return pl.pallas_call(
            _ln_kernel, grid=(m // tm,), in_specs=in_specs, out_specs=rows,
            out_shape=jax.ShapeDtypeStruct((m, d), F32),
            compiler_params=_cparams("parallel"), name="deepnorm_ln",
        )(*args)
    return pl.pallas_call(
        _ln_mod_kernel, grid=(m // tm,), in_specs=in_specs + [per_b, per_b],
        out_specs=[rows, rows],
        out_shape=[jax.ShapeDtypeStruct((m, d), F32), jax.ShapeDtypeStruct((m, d), BF16)],
        compiler_params=_cparams("parallel"), name="deepnorm_ln_mod",
    )(*args, *mod)


def _ffn_up_kernel(h_ref, wg_ref, wu_ref, wd_ref, o_ref, wdo_ref):
    tn = wg_ref.shape[1]
    w = jnp.concatenate([wg_ref[...].astype(BF16), wu_ref[...].astype(BF16)], axis=1)
    for rs in _row_chunks(h_ref.shape[0]):
        r = _dot_f32(h_ref[rs, :], w)
        o_ref[rs, :] = (_silu(r[:, :tn]) * r[:, tn:]).astype(o_ref.dtype)
    wdo_ref[...] = wd_ref[...].astype(wdo_ref.dtype)


def _ffn_up(h2, w_gate, w_up, w_down, tm=2048, tn=256):
    m, d = h2.shape
    n = w_gate.shape[1]
    tm = min(tm, m)
    nj = n // tn
    steps = (m // tm) * nj
    slab, rem = divmod(w_down.shape[0], steps)
    assert rem == 0 and slab % HALO_ROWS == 0, (w_down.shape, steps)
    wspec = pl.BlockSpec((d, tn), lambda i, j: (0, j))
    slab_spec = pl.BlockSpec((slab, w_down.shape[1]), lambda i, j: (i * nj + j, 0))
    return pl.pallas_call(
        _ffn_up_kernel,
        grid=(m // tm, nj),
        in_specs=[_resident((tm, d), lambda i, j: (i, 0)), wspec, wspec, slab_spec],
        out_specs=[pl.BlockSpec((tm, tn), lambda i, j: (i, j)), slab_spec],
        out_shape=[jax.ShapeDtypeStruct((m, n), BF16),
                   jax.ShapeDtypeStruct(w_down.shape, BF16)],
        compiler_params=_cparams("parallel", "arbitrary"),
        name="ffn_up",
    )(h2, w_gate, w_up, w_down)


def _ffn_down_kernel(a_ref, w_ref, x_ref, gate_ref, o_ref):
    o_ref[...] = ALPHA * x_ref[...] + gate_ref[0] * _dot_f32(a_ref[...], w_ref[...])


def _ffn_down(act, w_down, x1, gate, seq, tm=512, tn=512):
    m, k = act.shape
    n = w_down.shape[1]
    tm = min(tm, seq)
    per_batch = seq // tm
    tile = pl.BlockSpec((tm, tn), lambda i, j: (i, j))
    return pl.pallas_call(
        _ffn_down_kernel,
        grid=(m // tm, n // tn),
        in_specs=[pl.BlockSpec((tm, k), lambda i, j: (i, 0)),
                  pl.BlockSpec((k, tn), lambda i, j: (0, j)),
                  tile,
                  pl.BlockSpec((1, 1, tn), lambda i, j: (i // per_batch, 0, j))],
        out_specs=tile,
        out_shape=jax.ShapeDtypeStruct((m, n), F32),
        compiler_params=_cparams("parallel", "arbitrary"),
        name="ffn_down",
    )(act, w_down, x1, gate)


def _rope_tables(positions):
    half = ROPE_DIM // 2
    inv_freq = ROPE_THETA ** (-jnp.arange(0, ROPE_DIM, 2, dtype=F32) / ROPE_DIM)
    ang = positions.astype(F32)[..., None] * inv_freq
    cos, sin = jnp.cos(ang), jnp.sin(ang)
    rest = HEAD_DIM - ROPE_DIM
    shp = ang.shape[:-1]
    cos_t = jnp.concatenate([cos, cos, jnp.ones(shp + (rest,), F32)], axis=-1)
    s1_t = jnp.concatenate([-sin, jnp.zeros(shp + (HEAD_DIM - half,), F32)], axis=-1)
    s2_t = jnp.concatenate([jnp.zeros(shp + (half,), F32), sin, jnp.zeros(shp + (rest,), F32)], axis=-1)
    return cos_t, s1_t, s2_t


def kernel(x, c, positions, w_ada, b_ada, w_in, conv_w, conv_b, attn_sink, a_log_fwd, a_log_bwd,
           dt_bias_fwd, dt_bias_bwd, ssd_d, ssd_norm_w, attn_norm_w, w_out, ln1_g, ln1_b,
           w_gate, w_up, w_down, ln2_g, ln2_b):
    assert w_ada.shape[0] == 1, "single-layer block"
    bsz, seq, d = x.shape
    m = bsz * seq
    x2 = x.reshape(m, d)

    b_ada2 = b_ada[0].reshape(1, 6 * d)
    mod_head, cond_t = _ada_head(c, w_ada[0], b_ada2, 2 * d)
    sh1, sc1 = [t.reshape(bsz, 1, d) for t in jnp.split(mod_head, 2, axis=-1)]

    proj, dt_raw, mod_tail = _inproj(_modulate(x2, sc1, sh1, seq), jnp.swapaxes(w_in[0], 0, 1),
                                     cond_t, w_ada[0], b_ada2, 2 * d)
    g1, sh2, sc2, g2 = [t.reshape(bsz, 1, d) for t in jnp.split(mod_tail, 4, axis=-1)]
    proj3 = proj.reshape(bsz, seq, MAIN_WIDTH)

    qr, kr = _rope(proj3, *_rope_tables(positions))
    attn = _attention(qr, kr, proj3, attn_sink[0], attn_norm_w[0])

    conv_w8 = jnp.pad(conv_w[0], ((0, 8 - CONV_WIDTH), (0, 0)))
    xbc = _conv(proj3, conv_w8, conv_b[0])
    zeros = jnp.zeros((DT_LANES - 2 * SSD_HEADS,), F32)
    par = jnp.zeros((8, LANES), F32)
    par = par.at[0].set(jnp.concatenate([dt_bias_fwd[0], dt_bias_bwd[0], zeros]))
    par = par.at[1].set(jnp.concatenate([a_log_fwd[0], a_log_bwd[0], zeros]))
    dt_raw3 = dt_raw.reshape(bsz, seq, DT_LANES)
    y_b = _ssd_bwd(xbc, dt_raw3, par)
    d_x = jnp.repeat(ssd_d[0], SSD_HEAD_DIM).reshape(SSD_GROUPS, 1, GROUP_WIDTH)
    ssd = _ssd_fwd_gate(xbc, dt_raw3, par, y_b, proj3, d_x,
                        ssd_norm_w[0].reshape(SSD_GROUPS, 1, GROUP_WIDTH))

    v1 = _outproj(attn.reshape(m, ATTN_WIDTH), ssd.reshape(m, SSD_WIDTH), w_out[0], x2, g1, seq)
    x1, h2 = _deepnorm(v1, ln1_g[0], ln1_b[0], seq, mod=(sc2, sh2))

    act, w_down_b = _ffn_up(h2, w_gate[0], w_up[0], w_down[0])
    v2 = _ffn_down(act, w_down_b, x1, g2, seq)
    out = _deepnorm(v2, ln2_g[0], ln2_b[0], seq)
    return out.reshape(bsz, seq, d)
```
